```python
import math
import jax, jax.numpy as jnp
from jax import lax
import numpy as np

D_MODEL = 2048
BATCH = 8
SEQ = 2048
DEPTH = 1

GRID_W = 64
CTX_LEN = 256
HEAD_DIM = 128
N_HEADS = 8
N_KV_HEADS = 2
GROUP = N_HEADS // N_KV_HEADS
WINDOW = 128
BLOCK = 128
NBAND = -(-WINDOW // BLOCK)
BAND = (2 * NBAND + 1) * BLOCK
ROT_HALF = HEAD_DIM // 2
ROPE_BASE = 10000.0
SSM_WIDTH = D_MODEL // 4
SSM_GROUP = 16
SSM_GROUPS = SSM_WIDTH // SSM_GROUP
SSM_STATE = 64
D_FF = 4 * D_MODEL
Q_W = N_HEADS * HEAD_DIM
KV_W = N_KV_HEADS * HEAD_DIM
IN_COLS = Q_W + 2 * KV_W + SSM_WIDTH + 2 * D_MODEL
SPLIT_AT = (Q_W, Q_W + KV_W, Q_W + 2 * KV_W, Q_W + 2 * KV_W + SSM_WIDTH,
            Q_W + 2 * KV_W + SSM_WIDTH + D_MODEL)
ALPHA = (2.0 * DEPTH) ** 0.25
BETA = (8.0 * DEPTH) ** -0.25
LN_EPS = 1e-6
NEG_INF = -1e30

kernel_name = 'hybrid_dit_swa_s5_block'


def layer_norm(x):
    xf = x.astype(jnp.float32)
    mu = jnp.mean(xf, -1, keepdims=True)
    var = jnp.mean(jnp.square(xf - mu), -1, keepdims=True)
    return ((xf - mu) * lax.rsqrt(var + LN_EPS)).astype(x.dtype)


def post_norm(x, g, b):
    return layer_norm(x) * g + b


def axial_rope_tables(n_tokens):
    rows = n_tokens // GRID_W
    row = jnp.repeat(jnp.arange(rows), GRID_W)
    col = jnp.tile(jnp.arange(GRID_W), rows)
    n_freq = ROT_HALF // 2
    freqs = ROPE_BASE ** (-jnp.arange(n_freq, dtype=jnp.float32) / n_freq)
    ang_r = row.astype(jnp.float32)[:, None] * freqs
    ang_c = col.astype(jnp.float32)[:, None] * freqs
    ang = jnp.concatenate([ang_r, ang_r, ang_c, ang_c], -1)
    return jnp.cos(ang), jnp.sin(ang)


def rotate_axial(x):
    h = ROT_HALF // 2
    a, b = x[..., :ROT_HALF], x[..., ROT_HALF:]
    return jnp.concatenate([-a[..., h:], a[..., :h], -b[..., h:], b[..., :h]], -1)


def apply_rope(x, cos, sin):
    cos = cos[:, None, :].astype(x.dtype)
    sin = sin[:, None, :].astype(x.dtype)
    return x * cos + rotate_axial(x) * sin


def windowed_gqa_latent(q, k, v, k_ctx, v_ctx, sink):
    bsz, n_tok = q.shape[0], q.shape[1]
    n_ctx = k_ctx.shape[1]
    nb = n_tok // BLOCK
    scale = HEAD_DIM ** -0.5
    qb = q.reshape(bsz, nb, BLOCK, N_KV_HEADS, GROUP, HEAD_DIM)

    def band(t):
        tp = jnp.pad(t, ((0, 0), (NBAND * BLOCK, NBAND * BLOCK), (0, 0), (0, 0)))
        tp = tp.reshape(bsz, nb + 2 * NBAND, BLOCK, N_KV_HEADS, HEAD_DIM)
        return jnp.concatenate([tp[:, j:j + nb] for j in range(2 * NBAND + 1)], axis=2)

    kb, vb = band(k), band(v)
    q_pos = jnp.arange(n_tok).reshape(nb, BLOCK)
    k_pos = (jnp.arange(nb)[:, None] - NBAND) * BLOCK + jnp.arange(BAND)[None, :]
    valid = ((jnp.abs(q_pos[:, :, None] - k_pos[:, None, :]) <= WINDOW)
             & (k_pos >= 0)[:, None, :] & (k_pos < n_tok)[:, None, :])
    s_loc = jnp.einsum('bnqkgd,bnjkd->bnkgqj', qb, kb, preferred_element_type=jnp.float32) * scale
    s_loc = jnp.where(valid[None, :, None, None], s_loc, NEG_INF)
    s_ctx = jnp.einsum('bnqkgd,bckd->bnkgqc', qb, k_ctx, preferred_element_type=jnp.float32) * scale
    s_sink = jnp.broadcast_to(sink.astype(jnp.float32).reshape(1, 1, N_KV_HEADS, GROUP, 1, 1),
                              s_loc.shape[:-1] + (1,))
    p = jax.nn.softmax(jnp.concatenate([s_loc, s_ctx, s_sink], -1), axis=-1).astype(v.dtype)
    out = (jnp.einsum('bnkgqj,bnjkd->bnqkgd', p[..., :BAND], vb)
           + jnp.einsum('bnkgqc,bckd->bnqkgd', p[..., BAND:BAND + n_ctx], v_ctx))
    return out.reshape(bsz, n_tok, Q_W)


def context_attention(q, k, v, sink):
    bsz, n_ctx = q.shape[0], q.shape[1]
    scale = HEAD_DIM ** -0.5
    qg = q.reshape(bsz, n_ctx, N_KV_HEADS, GROUP, HEAD_DIM)
    s = jnp.einsum('bqkgd,bckd->bkgqc', qg, k, preferred_element_type=jnp.float32) * scale
    s_sink = jnp.broadcast_to(sink.astype(jnp.float32).reshape(1, N_KV_HEADS, GROUP, 1, 1),
                              s.shape[:-1] + (1,))
    p = jax.nn.softmax(jnp.concatenate([s, s_sink], -1), axis=-1).astype(v.dtype)
    out = jnp.einsum('bkgqc,bckd->bqkgd', p[..., :n_ctx], v)
    return out.reshape(bsz, n_ctx, Q_W)


def s5_discretise(a_re, a_im, log_dt, b_re, b_im):
    lam = lax.complex(a_re.astype(jnp.float32), a_im.astype(jnp.float32))
    dt = jnp.exp(log_dt.astype(jnp.float32))[:, None]
    lam_bar = jnp.exp(lam * dt)
    b = lax.complex(b_re.astype(jnp.float32), b_im.astype(jnp.float32))
    b_bar = ((lam_bar - 1.0) / lam)[..., None] * b
    return lam_bar, b_bar


def _linear_recurrence(e1, e2):
    a1, b1 = e1
    a2, b2 = e2
    return a1 * a2, a2 * b1 + b2


def s5_scan(u, lam_bar, b_bar, reverse, s0=None):
    bu = jnp.einsum('blgh,gph->blgp', u, b_bar)
    if s0 is not None:
        edge = u.shape[1] - 1 if reverse else 0
        bu = bu.at[:, edge].add(lam_bar * s0)
    a = jnp.broadcast_to(lam_bar, bu.shape)
    _, s = lax.associative_scan(_linear_recurrence, (a, bu), reverse=reverse, axis=1)
    return s


def s5_readout(s, c_re, c_im):
    cmat = lax.complex(c_re.astype(jnp.float32), c_im.astype(jnp.float32))
    return jnp.real(jnp.einsum('blgp,ghp->blgh', s, cmat))


def s5_bidirectional(u_lat, u_ctx, a_re, a_im, log_dt, b_re, b_im, c_re, c_im, d_skip, need_ctx):
    ul = u_lat.astype(jnp.float32).reshape(u_lat.shape[0], u_lat.shape[1], SSM_GROUPS, SSM_GROUP)
    uc = u_ctx.astype(jnp.float32).reshape(u_ctx.shape[0], u_ctx.shape[1], SSM_GROUPS, SSM_GROUP)
    dsk = d_skip.astype(jnp.float32)
    y_lat = dsk * ul
    y_ctx = dsk * uc
    for direction, reverse in enumerate((False, True)):
        lam_bar, b_bar = s5_discretise(a_re[direction], a_im[direction], log_dt[direction],
                                       b_re[direction], b_im[direction])
        s_ctx = s5_scan(uc, lam_bar, b_bar, reverse)
        s_init = s_ctx[:, 0] if reverse else s_ctx[:, -1]
        s_lat = s5_scan(ul, lam_bar, b_bar, reverse, s_init)
        y_lat = y_lat + s5_readout(s_lat, c_re[direction], c_im[direction])
        if need_ctx:
            y_ctx = y_ctx + s5_readout(s_ctx, c_re[direction], c_im[direction])
    y_lat = y_lat.reshape(u_lat.shape).astype(u_lat.dtype)
    y_ctx = y_ctx.reshape(u_ctx.shape).astype(u_ctx.dtype) if need_ctx else None
    return y_lat, y_ctx


def gelu_glu(y, w_glu):
    z = jax.nn.gelu(y) @ w_glu
    return z[..., :SSM_WIDTH] * jax.nn.sigmoid(z[..., SSM_WIDTH:])


def hybrid_mixer(h_lat, h_ctx, w_in, attn_sink, a_re, a_im, log_dt, b_re, b_im, c_re, c_im, d_skip,
                 w_glu, w_attn_up, w_ssm_up, w_out, need_ctx):
    q_l, k_l, v_l, u_l, ga_l, gs_l = jnp.split(h_lat @ w_in, SPLIT_AT, axis=-1)
    q_c, k_c, v_c, u_c, ga_c, gs_c = jnp.split(h_ctx @ w_in, SPLIT_AT, axis=-1)

    def heads(t, n):
        return t.reshape(t.shape[0], t.shape[1], n, HEAD_DIM)

    cos, sin = axial_rope_tables(h_lat.shape[1])
    q_l = apply_rope(heads(q_l, N_HEADS), cos, sin)
    k_l = apply_rope(heads(k_l, N_KV_HEADS), cos, sin)
    k_c, v_c = heads(k_c, N_KV_HEADS), heads(v_c, N_KV_HEADS)
    attn_l = windowed_gqa_latent(q_l, k_l, heads(v_l, N_KV_HEADS), k_c, v_c, attn_sink)
    ssm_l, ssm_c = s5_bidirectional(u_l, u_c, a_re, a_im, log_dt, b_re, b_im, c_re, c_im, d_skip, need_ctx)

    def merge(attn, ssm, ga, gs):
        attn_d = attn @ w_attn_up
        ssm_d = gelu_glu(ssm, w_glu) @ w_ssm_up
        return (jax.nn.sigmoid(ga) * attn_d + jax.nn.sigmoid(gs) * ssm_d) @ w_out

    y_lat = merge(attn_l, ssm_l, ga_l, gs_l)
    y_ctx = None
    if need_ctx:
        attn_c = context_attention(heads(q_c, N_HEADS), k_c, v_c, attn_sink)
        y_ctx = merge(attn_c, ssm_c, ga_c, gs_c)
    return y_lat, y_ctx


def squared_relu_mlp(h, w1, b1, w2, b2):
    return jnp.square(jax.nn.relu(h @ w1 + b1)) @ w2 + b2


def setup_inputs(seed: int = 0) -> dict:
    key = jax.random.key(seed)
    ks = jax.random.split(key, 28)
    f32 = jnp.float32

    def nrm(k, shape, s):
        return jax.random.normal(k, shape, f32) * s

    G, P, HG = SSM_GROUPS, SSM_STATE, SSM_GROUP
    return {
        'x': nrm(ks[0], (BATCH, SEQ, D_MODEL), 1.0),
        'c': nrm(ks[1], (BATCH, D_MODEL), 1.0),
        'ctx': nrm(ks[2], (BATCH, CTX_LEN, D_MODEL), 1.0),
        'c_ctx': nrm(ks[3], (D_MODEL,), 1.0),
        'w_ada': nrm(ks[4], (DEPTH, D_MODEL, 6 * D_MODEL), D_MODEL ** -0.5),
        'b_ada': nrm(ks[5], (DEPTH, 6 * D_MODEL), 0.01),
        'w_in': nrm(ks[6], (DEPTH, D_MODEL, IN_COLS), D_MODEL ** -0.5),
        'attn_sink': nrm(ks[7], (DEPTH, N_HEADS), 0.5),
        'ssm_a_re': -0.5 + nrm(ks[8], (DEPTH, 2, G, P), 0.01),
        'ssm_a_im': jnp.pi * jnp.arange(P, dtype=f32) + nrm(ks[9], (DEPTH, 2, G, P), 0.01),
        'ssm_log_dt': jax.random.uniform(ks[10], (DEPTH, 2, G), f32, math.log(1e-3), math.log(1e-1)),
        'ssm_b_re': nrm(ks[11], (DEPTH, 2, G, P, HG), (2 * HG) ** -0.5),
        'ssm_b_im': nrm(ks[12], (DEPTH, 2, G, P, HG), (2 * HG) ** -0.5),
        'ssm_c_re': nrm(ks[13], (DEPTH, 2, G, HG, P), P ** -0.5),
        'ssm_c_im': nrm(ks[14], (DEPTH, 2, G, HG, P), P ** -0.5),
        'ssm_d': nrm(ks[15], (DEPTH, G, HG), 0.5),
        'w_glu': nrm(ks[16], (DEPTH, SSM_WIDTH, 2 * SSM_WIDTH), SSM_WIDTH ** -0.5),
        'w_attn_up': nrm(ks[17], (DEPTH, Q_W, D_MODEL), Q_W ** -0.5),
        'w_ssm_up': nrm(ks[18], (DEPTH, SSM_WIDTH, D_MODEL), SSM_WIDTH ** -0.5),
        'w_out': nrm(ks[19], (DEPTH, D_MODEL, D_MODEL), BETA * D_MODEL ** -0.5),
        'ln_mix_g': 1.0 + nrm(ks[20], (DEPTH, D_MODEL), 0.01),
        'ln_mix_b': nrm(ks[21], (DEPTH, D_MODEL), 0.01),
        'w_mlp1': nrm(ks[22], (DEPTH, D_MODEL, D_FF), D_MODEL ** -0.5),
        'b_mlp1': nrm(ks[23], (DEPTH, D_FF), 0.01),
        'w_mlp2': nrm(ks[24], (DEPTH, D_FF, D_MODEL), BETA * D_FF ** -0.5),
        'b_mlp2': nrm(ks[25], (DEPTH, D_MODEL), 0.01),
        'ln_mlp_g': 1.0 + nrm(ks[26], (DEPTH, D_MODEL), 0.01),
        'ln_mlp_b': nrm(ks[27], (DEPTH, D_MODEL), 0.01),
    }


def reference(x, c, ctx, c_ctx, w_ada, b_ada, w_in, attn_sink, ssm_a_re, ssm_a_im, ssm_log_dt,
              ssm_b_re, ssm_b_im, ssm_c_re, ssm_c_im, ssm_d, w_glu, w_attn_up, w_ssm_up, w_out,
              ln_mix_g, ln_mix_b, w_mlp1, b_mlp1, w_mlp2, b_mlp2, ln_mlp_g, ln_mlp_b):
    ctx_s = ctx
    for layer in range(DEPTH):
        last = layer == DEPTH - 1
        mod_lat = jax.nn.silu(c) @ w_ada[layer] + b_ada[layer]
        mod_ctx = jax.nn.silu(c_ctx) @ w_ada[layer] + b_ada[layer]
        sh1, sc1, g1, sh2, sc2, g2 = jnp.split(mod_lat[:, None, :], 6, axis=-1)
        csh1, csc1, cg1, csh2, csc2, cg2 = jnp.split(mod_ctx, 6, axis=-1)

        h_lat = layer_norm(x) * (1.0 + sc1) + sh1
        h_ctx = layer_norm(ctx_s) * (1.0 + csc1) + csh1
        y_lat, y_ctx = hybrid_mixer(h_lat, h_ctx, w_in[layer], attn_sink[layer],
                                    ssm_a_re[layer], ssm_a_im[layer], ssm_log_dt[layer],
                                    ssm_b_re[layer], ssm_b_im[layer], ssm_c_re[layer], ssm_c_im[layer],
                                    ssm_d[layer], w_glu[layer], w_attn_up[layer], w_ssm_up[layer],
                                    w_out[layer], not last)
        x = post_norm(ALPHA * x + g1 * y_lat, ln_mix_g[layer], ln_mix_b[layer])
        h2 = layer_norm(x) * (1.0 + sc2) + sh2
        x = post_norm(ALPHA * x + g2 * squared_relu_mlp(h2, w_mlp1[layer], b_mlp1[layer],
                                                        w_mlp2[layer], b_mlp2[layer]),
                      ln_mlp_g[layer], ln_mlp_b[layer])
        if not last:
            ctx_s = post_norm(ALPHA * ctx_s + cg1 * y_ctx, ln_mix_g[layer], ln_mix_b[layer])
            hc2 = layer_norm(ctx_s) * (1.0 + csc2) + csh2
            ctx_s = post_norm(ALPHA * ctx_s + cg2 * squared_relu_mlp(hc2, w_mlp1[layer], b_mlp1[layer],
                                                                   w_mlp2[layer], b_mlp2[layer]),
                              ln_mlp_g[layer], ln_mlp_b[layer])
    return x
```

```python
import functools

import jax
import jax.numpy as jnp
from jax import lax
from jax.experimental import pallas as pl
from jax.experimental.pallas import tpu as pltpu

F32 = jnp.float32
BF16 = jnp.bfloat16

GRID_W = 64
HEAD_DIM = 128
N_HEADS = 8
N_KV_HEADS = 2
GROUP = N_HEADS // N_KV_HEADS
WINDOW = 128
BLOCK = 128
BAND = 3 * BLOCK
ROT_HALF = HEAD_DIM // 2
ROPE_BASE = 10000.0
SSM_GROUP = 16
SSM_STATE = 64
LN_EPS = 1e-6
NEG_INF = -1e30
Q_W = N_HEADS * HEAD_DIM
KV_W = N_KV_HEADS * HEAD_DIM

S5_CHUNK = 16
S5_CW = S5_CHUNK * SSM_GROUP
S5_SW = 4 * SSM_STATE

V7X_VMEM_LIMIT_BYTES = 56 * 1024 * 1024
HI = lax.Precision.HIGHEST
NT_DIMS = (((1,), (1,)), ((), ()))


def _params(*semantics):
    return pltpu.CompilerParams(dimension_semantics=semantics, vmem_limit_bytes=V7X_VMEM_LIMIT_BYTES)


def _layer_norm(x):
    mu = jnp.mean(x, -1, keepdims=True)
    xc = x - mu
    var = jnp.mean(xc * xc, -1, keepdims=True)
    return xc * lax.rsqrt(var + LN_EPS)


def _sigmoid(x):
    return 1.0 / (1.0 + jnp.exp(-x))


def _mod_kernel(c_ref, w_ref, b_ref, o_ref):
    c = c_ref[...]
    s = (c * _sigmoid(c)).astype(BF16)
    o_ref[...] = jnp.dot(s, w_ref[...].astype(BF16), preferred_element_type=F32) + b_ref[...]


def _mod(cc, w_ada, b_ada, tn):
    rows, d = cc.shape
    n = w_ada.shape[1]
    return pl.pallas_call(
        _mod_kernel,
        grid=(n // tn,),
        in_specs=[pl.BlockSpec((rows, d), lambda j: (0, 0)),
                  pl.BlockSpec((d, tn), lambda j: (0, j)),
                  pl.BlockSpec((1, tn), lambda j: (0, j))],
        out_specs=pl.BlockSpec((rows, tn), lambda j: (0, j)),
        out_shape=jax.ShapeDtypeStruct((rows, n), F32),
        compiler_params=_params("arbitrary"),
        name="mod",
    )(cc, w_ada, b_ada)


def _s5prep_kernel(are_ref, aim_ref, ldt_ref, br_ref, bi_ref, cr_ref, ci_ref, d_ref,
                   m_ref, e_ref, f_ref, l_ref):
    ar = are_ref[0]
    ai = aim_ref[0]
    dt = jnp.exp(ldt_ref[0])
    dar = dt * ar
    dai = dt * ai
    lane = lax.broadcasted_iota(jnp.int32, (S5_CHUNK, 128), 1)
    jj = lax.broadcasted_iota(jnp.int32, (S5_CHUNK, 128), 0)
    fwd = lane < SSM_STATE

    def powers(k):
        kf = k.astype(F32)
        mag = jnp.exp(kf * dar)
        ang = kf * dai
        return mag * jnp.cos(ang), mag * jnp.sin(ang)

    l1r, l1i = powers(jnp.ones((1, 128), jnp.int32))
    x = l1r - 1.0
    den = ar * ar + ai * ai
    cfr = (x * ar + l1i * ai) / den
    cfi = (l1i * ar - x * ai) / den
    br = br_ref[0]
    bi = bi_ref[0]
    bbr = cfr * br - cfi * bi
    bbi = cfr * bi + cfi * br
    cr = cr_ref[0]
    ci = ci_ref[0]

    def expand(tab):
        return jnp.concatenate(
            [jnp.broadcast_to(tab[j:j + 1], (SSM_GROUP, 128)) for j in range(S5_CHUNK)], 0)

    def tile(xm):
        return jnp.concatenate([xm] * S5_CHUNK, 0)

    def cprod(k, xr, xi):
        tr, ti = powers(k)
        tr, ti = expand(tr), expand(ti)
        xr, xi = tile(xr), tile(xi)
        return tr * xr - ti * xi, tr * xi + ti * xr

    last = S5_CHUNK - 1
    er, ei = cprod(jnp.where(fwd, last - jj, jj), bbr, bbi)
    e_ref[0] = jnp.concatenate([er, ei], 1)
    fr, fi = cprod(jnp.where(fwd, jj + 1, S5_CHUNK - jj), cr, ci)
    f_ref[0] = jnp.concatenate([fr, -fi], 1)
    lr, li = powers(jnp.full((1, 128), S5_CHUNK, jnp.int32))
    l_ref[0] = jnp.concatenate([lr, li], 1)

    rr, ri = cprod(jnp.where(fwd, jj, last - jj), cr, ci)
    rt = jnp.concatenate([rr, -ri], 1)
    bb = jnp.concatenate([bbr, bbi], 1)
    fwd2 = jnp.concatenate([fwd, fwd], 1)
    kf = lax.dot_general(jnp.where(fwd2, bb, 0.0), rt, NT_DIMS, precision=HI, preferred_element_type=F32)
    kb = lax.dot_general(jnp.where(fwd2, 0.0, bb), rt, NT_DIMS, precision=HI, preferred_element_type=F32)

    lane2 = lax.broadcasted_iota(jnp.int32, (SSM_GROUP, S5_CW), 1)
    hh = lax.broadcasted_iota(jnp.int32, (SSM_GROUP, S5_CW), 0)
    dsk = d_ref[0]
    for s in range(S5_CHUNK):
        lo = SSM_GROUP * s
        hi = SSM_GROUP * (s + 1)
        kfs = kf if s == 0 else pltpu.roll(kf, lo, 1)
        kbs = kb if s == last else pltpu.roll(kb, hi, 1)
        blk = jnp.where(lane2 >= lo, kfs, 0.0) + jnp.where(lane2 < hi, kbs, 0.0)
        blk = blk + jnp.where(lane2 == lo + hh, dsk, 0.0)
        m_ref[0, lo:hi, :] = blk


def _s5prep(a_re, a_im, log_dt, b_re, b_im, c_re, c_im, d_skip):
    g = a_re.shape[1]

    def fb(t):
        return jnp.concatenate([t[0], t[1]], -1)

    are = fb(a_re[:, :, None, :])
    aim = fb(a_im[:, :, None, :])
    ldt = fb(jnp.broadcast_to(log_dt[:, :, None, None], (2, g, 1, SSM_STATE)))
    br = fb(jnp.swapaxes(b_re, -1, -2))
    bi = fb(jnp.swapaxes(b_im, -1, -2))
    cr = fb(c_re)
    ci = fb(c_im)
    dsk = jnp.tile(d_skip, (1, S5_CHUNK))[:, None, :]

    row = pl.BlockSpec((1, 1, 128), lambda i: (i, 0, 0))
    mat = pl.BlockSpec((1, SSM_GROUP, 128), lambda i: (i, 0, 0))
    sq = pl.BlockSpec((1, S5_CW, S5_SW), lambda i: (i, 0, 0))
    return pl.pallas_call(
        _s5prep_kernel,
        grid=(g,),
        in_specs=[row, row, row, mat, mat, mat, mat, pl.BlockSpec((1, 1, S5_CW), lambda i: (i, 0, 0))],
        out_specs=[sq, sq, sq, pl.BlockSpec((1, 1, S5_SW), lambda i: (i, 0, 0))],
        out_shape=[jax.ShapeDtypeStruct((g, S5_CW, S5_CW), F32),
                   jax.ShapeDtypeStruct((g, S5_CW, S5_SW), F32),
                   jax.ShapeDtypeStruct((g, S5_CW, S5_SW), F32),
                   jax.ShapeDtypeStruct((g, 1, S5_SW), F32)],
        compiler_params=_params("arbitrary"),
        name="s5prep",
    )(are, aim, ldt, br, bi, cr, ci, dsk)


def _s5_kernel(u_ref, m_ref, e_ref, f_ref, l_ref, y_ref, s_ref, sf_ref, sb_ref, *, bsz, n_ctx, n_lat):
    u = u_ref[0]
    s_ref[...] = jnp.dot(u, e_ref[0], precision=HI, preferred_element_type=F32)
    half = S5_SW // 2
    lr = jnp.broadcast_to(l_ref[0][:, :half], (bsz, half))
    li = jnp.broadcast_to(l_ref[0][:, half:], (bsz, half))
    fwd = lax.broadcasted_iota(jnp.int32, (bsz, half), 1) < SSM_STATE
    n_all = n_ctx + n_lat

    def rows(c):
        return pl.ds(pl.multiple_of(c * bsz, bsz), bsz)

    def advance(cf, cb, carry):
        c_re, c_im = carry
        sf = s_ref[rows(cf), :]
        sb = s_ref[rows(cb), :]
        x_re = jnp.where(fwd, sf[:, :half], sb[:, :half])
        x_im = jnp.where(fwd, sf[:, half:], sb[:, half:])
        return lr * c_re - li * c_im + x_re, lr * c_im + li * c_re + x_im

    def ctx_step(i, carry):
        return advance(i, n_ctx - 1 - i, carry)

    def lat_step(i, carry):
        c_re, c_im = carry
        both = jnp.concatenate([c_re, c_im], 1)
        sf_ref[rows(i), :] = both
        sb_ref[rows(n_lat - 1 - i), :] = both
        return advance(n_ctx + i, n_all - 1 - i, carry)

    zero = jnp.zeros((bsz, half), F32)
    carry = lax.fori_loop(0, n_ctx, ctx_step, (zero, zero))
    lax.fori_loop(0, n_lat, lat_step, carry)

    fwd2 = (lax.broadcasted_iota(jnp.int32, (1, S5_SW), 1) % half) < SSM_STATE
    s_in = jnp.where(fwd2, sf_ref[...], sb_ref[...])
    u_lat = u_ref[0, n_ctx * bsz:, :]
    y_ref[0] = (jnp.dot(u_lat, m_ref[0], precision=HI, preferred_element_type=F32)
                + lax.dot_general(s_in, f_ref[0], NT_DIMS, precision=HI, preferred_element_type=F32))


def _s5(ut, m, e, f, l16, bsz, n_ctx, n_lat):
    g, rows_all, _ = ut.shape
    rows_lat = n_lat * bsz
    sq = pl.BlockSpec((1, S5_CW, S5_SW), lambda i: (i, 0, 0))
    return pl.pallas_call(
        functools.partial(_s5_kernel, bsz=bsz, n_ctx=n_ctx, n_lat=n_lat),
        grid=(g,),
        in_specs=[pl.BlockSpec((1, rows_all, S5_CW), lambda i: (i, 0, 0)), sq, sq, sq,
                  pl.BlockSpec((1, 1, S5_SW), lambda i: (i, 0, 0))],
        out_specs=pl.BlockSpec((1, rows_lat, S5_CW), lambda i: (i, 0, 0)),
        out_shape=jax.ShapeDtypeStruct((g, rows_lat, S5_CW), F32),
        scratch_shapes=[pltpu.VMEM((rows_all, S5_SW), F32),
                        pltpu.VMEM((rows_lat, S5_SW), F32),
                        pltpu.VMEM((rows_lat, S5_SW), F32)],
        compiler_params=_params("arbitrary"),
        name="s5",
    )(ut, m, e, f, l16)


def _to_chunks(u, bsz, groups):
    n = u.shape[0] // bsz // S5_CHUNK
    u = u.reshape(bsz, n, S5_CHUNK, groups, SSM_GROUP)
    return u.transpose(3, 1, 0, 2, 4).reshape(groups, n * bsz, S5_CW)


def _from_chunks(y, bsz, groups):
    n = y.shape[1] // bsz
    y = y.reshape(groups, n, bsz, S5_CHUNK, SSM_GROUP)
    return y.transpose(2, 1, 3, 0, 4).reshape(bsz * n * S5_CHUNK, groups * SSM_GROUP)


def _rope(x, cos, sin_lo, sin_hi):
    return (x * cos + pltpu.roll(x, HEAD_DIM - ROT_HALF // 2, 1) * sin_lo
            + pltpu.roll(x, ROT_HALF // 2, 1) * sin_hi)


def _inproj_kernel(x_ref, sc_ref, sh_ref, w_ref, cos_ref, slo_ref, shi_ref, o_ref, h_ref, *, tn):
    j = pl.program_id(1)

    @pl.when(j == 0)
    def _():
        h = _layer_norm(x_ref[...]) * (1.0 + sc_ref[0]) + sh_ref[0]
        h_ref[...] = h.astype(BF16)

    acc = jnp.dot(h_ref[...], w_ref[...], preferred_element_type=F32)
    q_tiles = Q_W // tn
    scale = HEAD_DIM ** -0.5

    @pl.when(j < q_tiles)
    def _():
        cos, slo, shi = cos_ref[...], slo_ref[...], shi_ref[...]
        for hd in range(tn // HEAD_DIM):
            sl = slice(hd * HEAD_DIM, (hd + 1) * HEAD_DIM)
            o_ref[:, sl] = _rope(acc[:, sl], cos, slo, shi) * scale

    @pl.when(j == q_tiles)
    def _():
        cos, slo, shi = cos_ref[...], slo_ref[...], shi_ref[...]
        for hd in range(N_KV_HEADS):
            sl = slice(hd * HEAD_DIM, (hd + 1) * HEAD_DIM)
            o_ref[:, sl] = _rope(acc[:, sl], cos, slo, shi)
        o_ref[:, KV_W:] = acc[:, KV_W:]

    @pl.when(j > q_tiles)
    def _():
        o_ref[...] = _sigmoid(acc)


def _inproj(x2, sc, sh, w_in_bf, cos, slo, shi, tm, tn, seq):
    n_tok, d = x2.shape
    n_cols = w_in_bf.shape[1]
    per_b = seq // tm
    mod_spec = pl.BlockSpec((1, 1, d), lambda i, j: (i // per_b, 0, 0))
    tab_spec = pl.BlockSpec((tm, HEAD_DIM), lambda i, j: (i % per_b, 0))
    return pl.pallas_call(
        functools.partial(_inproj_kernel, tn=tn),
        grid=(n_tok // tm, n_cols // tn),
        in_specs=[pl.BlockSpec((tm, d), lambda i, j: (i, 0)), mod_spec, mod_spec,
                  pl.BlockSpec((d, tn), lambda i, j: (0, j)), tab_spec, tab_spec, tab_spec],
        out_specs=pl.BlockSpec((tm, tn), lambda i, j: (i, j)),
        out_shape=jax.ShapeDtypeStruct((n_tok, n_cols), F32),
        scratch_shapes=[pltpu.VMEM((tm, d), BF16)],
        compiler_params=_params("arbitrary", "arbitrary"),
        name="inproj",
    )(x2, sc, sh, w_in_bf, cos, slo, shi)


def _ctxproj_kernel(x_ref, sc_ref, sh_ref, w_ref, o_ref):
    h = _layer_norm(x_ref[...]) * (1.0 + sc_ref[...]) + sh_ref[...]
    o_ref[...] = jnp.dot(h.astype(BF16), w_ref[...], preferred_element_type=F32)


def _ctxproj(ctx2, sc, sh, w_in_bf, tm, tn):
    n_tok, d = ctx2.shape
    vec = pl.BlockSpec((1, d), lambda i: (0, 0))
    return pl.pallas_call(
        _ctxproj_kernel,
        grid=(n_tok // tm,),
        in_specs=[pl.BlockSpec((tm, d), lambda i: (i, 0)), vec, vec,
                  pl.BlockSpec((d, tn), lambda i: (0, Q_W // tn))],
        out_specs=pl.BlockSpec((tm, tn), lambda i: (i, 0)),
        out_shape=jax.ShapeDtypeStruct((n_tok, tn), F32),
        compiler_params=_params("arbitrary"),
        name="ctxproj",
    )(ctx2, sc, sh, w_in_bf)


def _attn_kernel(sink_ref, q_ref, k_ref, v_ref, kc_ref, vc_ref, o_ref, *, seq):
    n = pl.program_id(1)
    start = pl.multiple_of(jnp.clip((n - 1) * BLOCK, 0, seq - BAND), BLOCK)
    kb = k_ref[pl.ds(start, BAND), :].astype(BF16)
    vb = v_ref[pl.ds(start, BAND), :].astype(BF16)
    kc = kc_ref[...].astype(BF16)
    vc = vc_ref[...].astype(BF16)
    rows = GROUP * BLOCK
    q_pos = n * BLOCK + lax.broadcasted_iota(jnp.int32, (rows, BAND), 0) % BLOCK
    k_pos = start + lax.broadcasted_iota(jnp.int32, (rows, BAND), 1)
    valid = jnp.abs(q_pos - k_pos) <= WINDOW
    for kv in range(N_KV_HEADS):
        ksl = slice(kv * HEAD_DIM, (kv + 1) * HEAD_DIM)
        heads = [kv * GROUP + g for g in range(GROUP)]
        q = jnp.concatenate([q_ref[:, h * HEAD_DIM:(h + 1) * HEAD_DIM] for h in heads], 0).astype(BF16)
        s_loc = lax.dot_general(q, kb[:, ksl], NT_DIMS, preferred_element_type=F32)
        s_loc = jnp.where(valid, s_loc, NEG_INF)
        s_ctx = lax.dot_general(q, kc[:, ksl], NT_DIMS, preferred_element_type=F32)
        sink = jnp.concatenate([jnp.full((BLOCK, 1), sink_ref[h], F32) for h in heads], 0)
        m = jnp.maximum(jnp.maximum(jnp.max(s_loc, -1, keepdims=True), jnp.max(s_ctx, -1, keepdims=True)), sink)
        p_loc = jnp.exp(s_loc - m)
        p_ctx = jnp.exp(s_ctx - m)
        den = jnp.sum(p_loc, -1, keepdims=True) + jnp.sum(p_ctx, -1, keepdims=True) + jnp.exp(sink - m)
        o = (jnp.dot(p_loc.astype(BF16), vb[:, ksl], preferred_element_type=F32)
             + jnp.dot(p_ctx.astype(BF16), vc[:, ksl], preferred_element_type=F32)) / den
        for g, h in enumerate(heads):
            o_ref[:, h * HEAD_DIM:(h + 1) * HEAD_DIM] = o[g * BLOCK:(g + 1) * BLOCK].astype(o_ref.dtype)


def _attn(sink, proj, cproj, bsz, seq, n_ctx_tok):
    nb = seq // BLOCK
    kcol = Q_W // KV_W
    return pl.pallas_call(
        functools.partial(_attn_kernel, seq=seq),
        grid_spec=pltpu.PrefetchScalarGridSpec(
            num_scalar_prefetch=1,
            grid=(bsz, nb),
            in_specs=[pl.BlockSpec((BLOCK, Q_W), lambda b, n, s: (b * nb + n, 0)),
                      pl.BlockSpec((seq, KV_W), lambda b, n, s: (b, kcol)),
                      pl.BlockSpec((seq, KV_W), lambda b, n, s: (b, kcol + 1)),
                      pl.BlockSpec((n_ctx_tok, KV_W), lambda b, n, s: (b, 0)),
                      pl.BlockSpec((n_ctx_tok, KV_W), lambda b, n, s: (b, 1))],
            out_specs=pl.BlockSpec((BLOCK, Q_W), lambda b, n, s: (b * nb + n, 0)),
        ),
        out_shape=jax.ShapeDtypeStruct((bsz * seq, Q_W), BF16),
        compiler_params=_params("arbitrary", "arbitrary"),
        name="attn",
    )(sink, proj, proj, proj, cproj, cproj)


def _gelu_tanh(x):
    return 0.5 * x * (1.0 + jnp.tanh(0.7978845608028654 * (x + 0.044715 * (x * x * x))))


def _mix_kernel(a_ref, y_ref, ga_ref, gs_ref, wg_ref, wa_ref, ws_ref, o_ref, *, ssm_w):
    z = jnp.dot(_gelu_tanh(y_ref[...]).astype(BF16), wg_ref[...], preferred_element_type=F32)
    glu = (z[:, :ssm_w] * _sigmoid(z[:, ssm_w:])).astype(BF16)
    attn_d = jnp.dot(a_ref[...], wa_ref[...], preferred_element_type=F32)
    ssm_d = jnp.dot(glu, ws_ref[...], preferred_element_type=F32)
    o_ref[...] = (ga_ref[...] * attn_d + gs_ref[...] * ssm_d).astype(o_ref.dtype)


def _mix(attn, y_ssm, proj, w_glu_bf, w_au_bf, w_su_bf, tm, d):
    n_tok = attn.shape[0]
    ssm_w = y_ssm.shape[1]
    ga_col = (Q_W + 2 * KV_W + ssm_w) // d
    full = lambda a: pl.BlockSpec(a.shape, lambda i: (0, 0))
    return pl.pallas_call(
        functools.partial(_mix_kernel, ssm_w=ssm_w),
        grid=(n_tok // tm,),
        in_specs=[pl.BlockSpec((tm, Q_W), lambda i: (i, 0)),
                  pl.BlockSpec((tm, ssm_w), lambda i: (i, 0)),
                  pl.BlockSpec((tm, d), lambda i: (i, ga_col)),
                  pl.BlockSpec((tm, d), lambda i: (i, ga_col + 1)),
                  full(w_glu_bf), full(w_au_bf), full(w_su_bf)],
        out_specs=pl.BlockSpec((tm, d), lambda i: (i, 0)),
        out_shape=jax.ShapeDtypeStruct((n_tok, d), BF16),
        compiler_params=_params("arbitrary"),
        name="mix",
    )(attn, y_ssm, proj, proj, w_glu_bf, w_au_bf, w_su_bf)


def _outproj_kernel(m_ref, x_ref, w_ref, g1_ref, lg_ref, lb_ref, sc_ref, sh_ref, x1_ref, h2_ref, *, alpha):
    y = jnp.dot(m_ref[...], w_ref[...], preferred_element_type=F32)
    x1 = _layer_norm(alpha * x_ref[...] + g1_ref[0] * y) * lg_ref[...] + lb_ref[...]
    x1_ref[...] = x1
    h2_ref[...] = (_layer_norm(x1) * (1.0 + sc_ref[0]) + sh_ref[0]).astype(h2_ref.dtype)


def _outproj(mixed, x2, w_out_bf, g1, ln_g, ln_b, sc2, sh2, tm, seq, alpha):
    n_tok, d = x2.shape
    per_b = seq // tm
    rows = pl.BlockSpec((tm, d), lambda i: (i, 0))
    mod_spec = pl.BlockSpec((1, 1, d), lambda i: (i // per_b, 0, 0))
    vec = pl.BlockSpec((1, d), lambda i: (0, 0))
    return pl.pallas_call(
        functools.partial(_outproj_kernel, alpha=alpha),
        grid=(n_tok // tm,),
        in_specs=[rows, rows, pl.BlockSpec((d, d), lambda i: (0, 0)), mod_spec, vec, vec, mod_spec, mod_spec],
        out_specs=[rows, rows],
        out_shape=[jax.ShapeDtypeStruct((n_tok, d), F32), jax.ShapeDtypeStruct((n_tok, d), BF16)],
        compiler_params=_params("arbitrary"),
        name="outproj",
    )(mixed, x2, w_out_bf, g1, ln_g, ln_b, sc2, sh2)


def _mlp_kernel(h_ref, x_ref, w1_ref, b1_ref, w2_ref, b2_ref, g2_ref, lg_ref, lb_ref, o_ref, acc_ref, *, alpha):
    f = pl.program_id(1)
    a = jnp.dot(h_ref[...], w1_ref[...], preferred_element_type=F32) + b1_ref[...]
    a = jnp.maximum(a, 0.0)
    part = jnp.dot((a * a).astype(BF16), w2_ref[...], preferred_element_type=F32)

    @pl.when(f == 0)
    def _():
        acc_ref[...] = part

    @pl.when(f > 0)
    def _():
        acc_ref[...] += part

    @pl.when(f == pl.num_programs(1) - 1)
    def _():
        y = acc_ref[...] + b2_ref[...]
        o_ref[...] = _layer_norm(alpha * x_ref[...] + g2_ref[0] * y) * lg_ref[...] + lb_ref[...]


def _mlp(h2, x1, w1_bf, b1, w2_bf, b2, g2, ln_g, ln_b, tm, tf, seq, alpha):
    n_tok, d = x1.shape
    d_ff = w1_bf.shape[1]
    per_b = seq // tm
    rows = pl.BlockSpec((tm, d), lambda i, f: (i, 0))
    vec = pl.BlockSpec((1, d), lambda i, f: (0, 0))
    return pl.pallas_call(
        functools.partial(_mlp_kernel, alpha=alpha),
        grid=(n_tok // tm, d_ff // tf),
        in_specs=[rows, rows,
                  pl.BlockSpec((d, tf), lambda i, f: (0, f)),
                  pl.BlockSpec((1, tf), lambda i, f: (0, f)),
                  pl.BlockSpec((tf, d), lambda i, f: (f, 0)),
                  vec, pl.BlockSpec((1, 1, d), lambda i, f: (i // per_b, 0, 0)), vec, vec],
        out_specs=rows,
        out_shape=jax.ShapeDtypeStruct((n_tok, d), F32),
        scratch_shapes=[pltpu.VMEM((tm, d), F32)],
        compiler_params=_params("arbitrary", "arbitrary"),
        name="mlp",
    )(h2, x1, w1_bf, b1, w2_bf, b2, g2, ln_g, ln_b)


def _rope_tables(seq):
    rows = seq // GRID_W
    row = jnp.repeat(jnp.arange(rows), GRID_W)
    col = jnp.tile(jnp.arange(GRID_W), rows)
    n_freq = ROT_HALF // 2
    freqs = ROPE_BASE ** (-jnp.arange(n_freq, dtype=F32) / n_freq)
    ang_r = row.astype(F32)[:, None] * freqs
    ang_c = col.astype(F32)[:, None] * freqs
    ang = jnp.concatenate([ang_r, ang_r, ang_c, ang_c], -1)
    cos, sin = jnp.cos(ang), jnp.sin(ang)
    low = (jnp.arange(HEAD_DIM) % ROT_HALF) < n_freq
    return cos, jnp.where(low, -sin, 0.0), jnp.where(low, 0.0, sin)


def _tiles(seq):
    return dict(mod_tn=1024, proj_tm=min(1024, seq), proj_tn=1024, row_tm=min(512, seq), mlp_tf=1024)


def kernel(x, c, ctx, c_ctx, w_ada, b_ada, w_in, attn_sink, ssm_a_re, ssm_a_im, ssm_log_dt, ssm_b_re, ssm_b_im,
           ssm_c_re, ssm_c_im, ssm_d, w_glu, w_attn_up, w_ssm_up, w_out, ln_mix_g, ln_mix_b, w_mlp1, b_mlp1,
           w_mlp2, b_mlp2, ln_mlp_g, ln_mlp_b):
    depth = w_ada.shape[0]
    assert depth == 1, "single-layer problem: the context stream is never updated"
    bsz, seq, d = x.shape
    n_ctx_tok = ctx.shape[1]
    groups = ssm_a_re.shape[2]
    ssm_w = groups * SSM_GROUP
    assert w_in.shape[2] == Q_W + 2 * KV_W + ssm_w + 2 * d and Q_W + 2 * KV_W + ssm_w == d
    assert seq % BLOCK == 0 and seq >= BAND and n_ctx_tok % S5_CHUNK == 0 and bsz == 8
    alpha = (2.0 * depth) ** 0.25
    t = _tiles(seq)

    pad = jnp.zeros((16 - bsz - 1, d), F32)
    mod = _mod(jnp.concatenate([c, c_ctx[None], pad], 0), w_ada[0], b_ada[0][None], t["mod_tn"])
    sh1, sc1, g1, sh2, sc2, g2 = [mod[:bsz, i * d:(i + 1) * d][:, None, :] for i in range(6)]
    csh1, csc1 = mod[bsz:bsz + 1, :d], mod[bsz:bsz + 1, d:2 * d]

    w_in_bf = w_in[0].astype(BF16)
    x2 = x.reshape(bsz * seq, d)
    cos, slo, shi = _rope_tables(seq)
    proj = _inproj(x2, sc1, sh1, w_in_bf, cos, slo, shi, t["proj_tm"], t["proj_tn"], seq)
    cproj = _ctxproj(ctx.reshape(bsz * n_ctx_tok, d), csc1, csh1, w_in_bf, min(1024, bsz * n_ctx_tok), d // 2)

    attn = _attn(attn_sink[0], proj, cproj, bsz, seq, n_ctx_tok)

    m, e, f, l16 = _s5prep(ssm_a_re[0], ssm_a_im[0], ssm_log_dt[0], ssm_b_re[0], ssm_b_im[0],
                           ssm_c_re[0], ssm_c_im[0], ssm_d[0])
    u_off = Q_W + 2 * KV_W
    ut = jnp.concatenate([_to_chunks(cproj[:, 2 * KV_W:], bsz, groups),
                          _to_chunks(proj[:, u_off:u_off + ssm_w], bsz, groups)], 1)
    y_ssm = _from_chunks(_s5(ut, m, e, f, l16, bsz, n_ctx_tok // S5_CHUNK, seq // S5_CHUNK), bsz, groups)

    mixed = _mix(attn, y_ssm, proj, w_glu[0].astype(BF16), w_attn_up[0].astype(BF16), w_ssm_up[0].astype(BF16),
                 t["row_tm"], d)
    x1, h2 = _outproj(mixed, x2, w_out[0].astype(BF16), g1, ln_mix_g, ln_mix_b, sc2, sh2, t["row_tm"], seq, alpha)
    out = _mlp(h2, x1, w_mlp1[0].astype(BF16), b_mlp1, w_mlp2[0].astype(BF16), b_mlp2, g2, ln_mlp_g, ln_mlp_b,
               t["row_tm"], t["mlp_tf"], seq, alpha)
    return out.reshape(bsz, seq, d)
```

```python
import functools

import jax
import jax.numpy as jnp
from jax import lax
from jax.experimental import pallas as pl
from jax.experimental.pallas import tpu as pltpu

F32 = jnp.float32
BF16 = jnp.bfloat16

GRID_W = 64
HEAD_DIM = 128
N_HEADS = 8
N_KV_HEADS = 2
GROUP = N_HEADS // N_KV_HEADS
WINDOW = 128
BLOCK = 128
BAND = 3 * BLOCK
ROT_HALF = HEAD_DIM // 2
ROPE_BASE = 10000.0
SSM_GROUP = 16
SSM_STATE = 64
LN_EPS = 1e-6
NEG_INF = -1e30
Q_W = N_HEADS * HEAD_DIM
KV_W = N_KV_HEADS * HEAD_DIM

S5_CHUNK = 16
S5_CW = S5_CHUNK * SSM_GROUP
S5_SW = 4 * SSM_STATE

V7X_VMEM_LIMIT_BYTES = 56 * 1024 * 1024
HI = lax.Precision.HIGHEST
NT_DIMS = (((1,), (1,)), ((), ()))


def _params(*semantics):
    return pltpu.CompilerParams(dimension_semantics=semantics, vmem_limit_bytes=V7X_VMEM_LIMIT_BYTES)


def _layer_norm(x):
    mu = jnp.mean(x, -1, keepdims=True)
    xc = x - mu
    var = jnp.mean(xc * xc, -1, keepdims=True)
    return xc * lax.rsqrt(var + LN_EPS)


def _sigmoid(x):
    return 1.0 / (1.0 + jnp.exp(-x))


def _mod_kernel(c_ref, w_ref, b_ref, o_ref):
    c = c_ref[...]
    s = (c * _sigmoid(c)).astype(BF16)
    o_ref[...] = jnp.dot(s, w_ref[...].astype(BF16), preferred_element_type=F32) + b_ref[...]


def _mod(cc, w_ada, b_ada, tn):
    rows, d = cc.shape
    n = w_ada.shape[1]
    return pl.pallas_call(
        _mod_kernel,
        grid=(n // tn,),
        in_specs=[pl.BlockSpec((rows, d), lambda j: (0, 0)),
                  pl.BlockSpec((d, tn), lambda j: (0, j)),
                  pl.BlockSpec((1, tn), lambda j: (0, j))],
        out_specs=pl.BlockSpec((rows, tn), lambda j: (0, j)),
        out_shape=jax.ShapeDtypeStruct((rows, n), F32),
        compiler_params=_params("arbitrary"),
        name="mod",
    )(cc, w_ada, b_ada)


def _s5prep_kernel(are_ref, aim_ref, ldt_ref, br_ref, bi_ref, cr_ref, ci_ref, d_ref,
                   m_ref, e_ref, f_ref, l_ref):
    ar = are_ref[0]
    ai = aim_ref[0]
    dt = jnp.exp(ldt_ref[0])
    dar = dt * ar
    dai = dt * ai
    lane = lax.broadcasted_iota(jnp.int32, (S5_CHUNK, 128), 1)
    jj = lax.broadcasted_iota(jnp.int32, (S5_CHUNK, 128), 0)
    fwd = lane < SSM_STATE

    def powers(k):
        kf = k.astype(F32)
        mag = jnp.exp(kf * dar)
        ang = kf * dai
        return mag * jnp.cos(ang), mag * jnp.sin(ang)

    l1r, l1i = powers(jnp.ones((1, 128), jnp.int32))
    x = l1r - 1.0
    den = ar * ar + ai * ai
    cfr = (x * ar + l1i * ai) / den
    cfi = (l1i * ar - x * ai) / den
    br = br_ref[0]
    bi = bi_ref[0]
    bbr = cfr * br - cfi * bi
    bbi = cfr * bi + cfi * br
    cr = cr_ref[0]
    ci = ci_ref[0]

    def expand(tab):
        return jnp.concatenate(
            [jnp.broadcast_to(tab[j:j + 1], (SSM_GROUP, 128)) for j in range(S5_CHUNK)], 0)

    def tile(xm):
        return jnp.concatenate([xm] * S5_CHUNK, 0)

    def cprod(k, xr, xi):
        tr, ti = powers(k)
        tr, ti = expand(tr), expand(ti)
        xr, xi = tile(xr), tile(xi)
        return tr * xr - ti * xi, tr * xi + ti * xr

    last = S5_CHUNK - 1
    er, ei = cprod(jnp.where(fwd, last - jj, jj), bbr, bbi)
    e_ref[0] = jnp.concatenate([er, ei], 1)
    fr, fi = cprod(jnp.where(fwd, jj + 1, S5_CHUNK - jj), cr, ci)
    f_ref[0] = jnp.concatenate([fr, -fi], 1)
    lr, li = powers(jnp.full((1, 128), S5_CHUNK, jnp.int32))
    l_ref[0] = jnp.concatenate([lr, li], 1)

    rr, ri = cprod(jnp.where(fwd, jj, last - jj), cr, ci)
    rt = jnp.concatenate([rr, -ri], 1)
    bb = jnp.concatenate([bbr, bbi], 1)
    fwd2 = jnp.concatenate([fwd, fwd], 1)
    kf = lax.dot_general(jnp.where(fwd2, bb, 0.0), rt, NT_DIMS, precision=HI, preferred_element_type=F32)
    kb = lax.dot_general(jnp.where(fwd2, 0.0, bb), rt, NT_DIMS, precision=HI, preferred_element_type=F32)

    lane2 = lax.broadcasted_iota(jnp.int32, (SSM_GROUP, S5_CW), 1)
    hh = lax.broadcasted_iota(jnp.int32, (SSM_GROUP, S5_CW), 0)
    dsk = d_ref[0]
    for s in range(S5_CHUNK):
        lo = SSM_GROUP * s
        hi = SSM_GROUP * (s + 1)
        kfs = kf if s == 0 else pltpu.roll(kf, lo, 1)
        kbs = kb if s == last else pltpu.roll(kb, hi, 1)
        blk = jnp.where(lane2 >= lo, kfs, 0.0) + jnp.where(lane2 < hi, kbs, 0.0)
        blk = blk + jnp.where(lane2 == lo + hh, dsk, 0.0)
        m_ref[0, lo:hi, :] = blk


def _s5prep(a_re, a_im, log_dt, b_re, b_im, c_re, c_im, d_skip):
    g = a_re.shape[1]

    def fb(t):
        return jnp.concatenate([t[0], t[1]], -1)

    are = fb(a_re[:, :, None, :])
    aim = fb(a_im[:, :, None, :])
    ldt = fb(jnp.broadcast_to(log_dt[:, :, None, None], (2, g, 1, SSM_STATE)))
    br = fb(jnp.swapaxes(b_re, -1, -2))
    bi = fb(jnp.swapaxes(b_im, -1, -2))
    cr = fb(c_re)
    ci = fb(c_im)
    dsk = jnp.tile(d_skip, (1, S5_CHUNK))[:, None, :]

    row = pl.BlockSpec((1, 1, 128), lambda i: (i, 0, 0))
    mat = pl.BlockSpec((1, SSM_GROUP, 128), lambda i: (i, 0, 0))
    sq = pl.BlockSpec((1, S5_CW, S5_SW), lambda i: (i, 0, 0))
    return pl.pallas_call(
        _s5prep_kernel,
        grid=(g,),
        in_specs=[row, row, row, mat, mat, mat, mat, pl.BlockSpec((1, 1, S5_CW), lambda i: (i, 0, 0))],
        out_specs=[sq, sq, sq, pl.BlockSpec((1, 1, S5_SW), lambda i: (i, 0, 0))],
        out_shape=[jax.ShapeDtypeStruct((g, S5_CW, S5_CW), F32),
                   jax.ShapeDtypeStruct((g, S5_CW, S5_SW), F32),
                   jax.ShapeDtypeStruct((g, S5_CW, S5_SW), F32),
                   jax.ShapeDtypeStruct((g, 1, S5_SW), F32)],
        compiler_params=_params("arbitrary"),
        name="s5prep",
    )(are, aim, ldt, br, bi, cr, ci, dsk)


def _s5_kernel(uc_ref, ul_ref, m_ref, e_ref, f_ref, l_ref, y_ref, s_ref, sf_ref, sb_ref, *, bsz, n_ctx, n_lat):
    u_lat = ul_ref[0]
    s_ref[:n_ctx * bsz, :] = jnp.dot(uc_ref[0], e_ref[0], precision=HI, preferred_element_type=F32)
    s_ref[n_ctx * bsz:, :] = jnp.dot(u_lat, e_ref[0], precision=HI, preferred_element_type=F32)
    half = S5_SW // 2
    lr = jnp.broadcast_to(l_ref[0][:, :half], (bsz, half))
    li = jnp.broadcast_to(l_ref[0][:, half:], (bsz, half))
    fwd = lax.broadcasted_iota(jnp.int32, (bsz, half), 1) < SSM_STATE
    n_all = n_ctx + n_lat

    def rows(c):
        return pl.ds(pl.multiple_of(c * bsz, bsz), bsz)

    def advance(cf, cb, carry):
        c_re, c_im = carry
        sf = s_ref[rows(cf), :]
        sb = s_ref[rows(cb), :]
        x_re = jnp.where(fwd, sf[:, :half], sb[:, :half])
        x_im = jnp.where(fwd, sf[:, half:], sb[:, half:])
        return lr * c_re - li * c_im + x_re, lr * c_im + li * c_re + x_im

    def ctx_step(i, carry):
        return advance(i, n_ctx - 1 - i, carry)

    def lat_step(i, carry):
        c_re, c_im = carry
        both = jnp.concatenate([c_re, c_im], 1)
        sf_ref[rows(i), :] = both
        sb_ref[rows(n_lat - 1 - i), :] = both
        return advance(n_ctx + i, n_all - 1 - i, carry)

    zero = jnp.zeros((bsz, half), F32)
    carry = lax.fori_loop(0, n_ctx, ctx_step, (zero, zero))
    lax.fori_loop(0, n_lat, lat_step, carry)

    fwd2 = (lax.broadcasted_iota(jnp.int32, (1, S5_SW), 1) % half) < SSM_STATE
    s_in = jnp.where(fwd2, sf_ref[...], sb_ref[...])
    y_ref[0] = (jnp.dot(u_lat, m_ref[0], precision=HI, preferred_element_type=F32)
                + lax.dot_general(s_in, f_ref[0], NT_DIMS, precision=HI, preferred_element_type=F32))


def _s5(ut_ctx, ut_lat, m, e, f, l16, bsz):
    g, rows_ctx, _ = ut_ctx.shape
    rows_lat = ut_lat.shape[1]
    sq = pl.BlockSpec((1, S5_CW, S5_SW), lambda i: (i, 0, 0))
    return pl.pallas_call(
        functools.partial(_s5_kernel, bsz=bsz, n_ctx=rows_ctx // bsz, n_lat=rows_lat // bsz),
        grid=(g,),
        in_specs=[pl.BlockSpec((1, rows_ctx, S5_CW), lambda i: (i, 0, 0)),
                  pl.BlockSpec((1, rows_lat, S5_CW), lambda i: (i, 0, 0)), sq, sq, sq,
                  pl.BlockSpec((1, 1, S5_SW), lambda i: (i, 0, 0))],
        out_specs=pl.BlockSpec((1, rows_lat, S5_CW), lambda i: (i, 0, 0)),
        out_shape=jax.ShapeDtypeStruct((g, rows_lat, S5_CW), F32),
        scratch_shapes=[pltpu.VMEM((rows_ctx + rows_lat, S5_SW), F32),
                        pltpu.VMEM((rows_lat, S5_SW), F32),
                        pltpu.VMEM((rows_lat, S5_SW), F32)],
        compiler_params=_params("arbitrary"),
        name="s5",
    )(ut_ctx, ut_lat, m, e, f, l16)


SLAB = 128
SLAB_GROUPS = SLAB // SSM_GROUP
SLAB_W = S5_CHUNK * SLAB


def _rows_to_slab(src_ref, dst_ref):
    n_slab, n_chunks, _ = dst_ref.shape
    for t in range(S5_CHUNK):
        for s in range(n_slab):
            dst_ref[s, :, t * SLAB:(t + 1) * SLAB] = src_ref[s, pl.ds(t, n_chunks, stride=S5_CHUNK), :]


def _slab_to_rows(src_ref, dst_ref):
    n_slab, n_chunks, _ = src_ref.shape
    for t in range(S5_CHUNK):
        for s in range(n_slab):
            dst_ref[s, pl.ds(t, n_chunks, stride=S5_CHUNK), :] = src_ref[s, :, t * SLAB:(t + 1) * SLAB]


def _piece_transpose(tiles):
    piece = lax.broadcasted_iota(jnp.int32, tiles[0].shape, 1) // SSM_GROUP
    out = []
    for b in range(SLAB_GROUPS):
        acc = None
        for a in range(SLAB_GROUPS):
            shift = ((a - b) * SSM_GROUP) % SLAB
            r = tiles[a] if shift == 0 else pltpu.roll(tiles[a], shift, 1)
            acc = r if acc is None else jnp.where(piece == a, r, acc)
        out.append(acc)
    return out


def _split_kernel(x_ref, o_ref):
    x = x_ref[0]
    halves = []
    for th in range(S5_CHUNK // SLAB_GROUPS):
        tiles = [x[:, (th * SLAB_GROUPS + a) * SLAB:(th * SLAB_GROUPS + a + 1) * SLAB] for a in range(SLAB_GROUPS)]
        halves.append(_piece_transpose(tiles))
    for g in range(SLAB_GROUPS):
        o_ref[g] = jnp.concatenate([h[g] for h in halves], 1)


def _split(slabs, tr):
    n_slab, rows, _ = slabs.shape
    return pl.pallas_call(
        _split_kernel,
        grid=(n_slab, rows // tr),
        in_specs=[pl.BlockSpec((1, tr, SLAB_W), lambda s, r: (s, r, 0))],
        out_specs=pl.BlockSpec((SLAB_GROUPS, tr, S5_CW), lambda s, r: (s, r, 0)),
        out_shape=jax.ShapeDtypeStruct((n_slab * SLAB_GROUPS, rows, S5_CW), F32),
        compiler_params=_params("arbitrary", "arbitrary"),
        name="split",
    )(slabs)


def _merge_kernel(y_ref, o_ref):
    for th in range(S5_CHUNK // SLAB_GROUPS):
        tiles = [y_ref[g, :, th * SLAB:(th + 1) * SLAB] for g in range(SLAB_GROUPS)]
        for a, tile in enumerate(_piece_transpose(tiles)):
            t = th * SLAB_GROUPS + a
            o_ref[0, :, t * SLAB:(t + 1) * SLAB] = tile


def _merge(y, tr):
    g, rows, _ = y.shape
    return pl.pallas_call(
        _merge_kernel,
        grid=(g // SLAB_GROUPS, rows // tr),
        in_specs=[pl.BlockSpec((SLAB_GROUPS, tr, S5_CW), lambda s, r: (s, r, 0))],
        out_specs=pl.BlockSpec((1, tr, SLAB_W), lambda s, r: (s, r, 0)),
        out_shape=jax.ShapeDtypeStruct((g // SLAB_GROUPS, rows, SLAB_W), F32),
        compiler_params=_params("arbitrary", "arbitrary"),
        name="merge",
    )(y)


def _rope(x, cos, sin_lo, sin_hi):
    return (x * cos + pltpu.roll(x, HEAD_DIM - ROT_HALF // 2, 1) * sin_lo
            + pltpu.roll(x, ROT_HALF // 2, 1) * sin_hi)


def _inproj_kernel(x_ref, sc_ref, sh_ref, w_ref, cos_ref, slo_ref, shi_ref, o_ref, us_ref, h_ref, u_ref, *, tn):
    j = pl.program_id(1)

    @pl.when(j == 0)
    def _():
        h = _layer_norm(x_ref[...]) * (1.0 + sc_ref[0]) + sh_ref[0]
        h_ref[...] = h.astype(BF16)

    acc = jnp.dot(h_ref[...], w_ref[...], preferred_element_type=F32)
    q_tiles = Q_W // tn
    scale = HEAD_DIM ** -0.5

    @pl.when(j < q_tiles)
    def _():
        cos, slo, shi = cos_ref[...], slo_ref[...], shi_ref[...]
        for hd in range(tn // HEAD_DIM):
            sl = slice(hd * HEAD_DIM, (hd + 1) * HEAD_DIM)
            o_ref[:, sl] = _rope(acc[:, sl], cos, slo, shi) * scale

    @pl.when(j == q_tiles)
    def _():
        cos, slo, shi = cos_ref[...], slo_ref[...], shi_ref[...]
        for hd in range(N_KV_HEADS):
            sl = slice(hd * HEAD_DIM, (hd + 1) * HEAD_DIM)
            o_ref[:, sl] = _rope(acc[:, sl], cos, slo, shi)
        o_ref[:, KV_W:] = acc[:, KV_W:]
        for s in range(u_ref.shape[0]):
            u_ref[s] = acc[:, 2 * KV_W + s * SLAB:2 * KV_W + (s + 1) * SLAB]
        _rows_to_slab(u_ref, us_ref)

    @pl.when(j > q_tiles)
    def _():
        o_ref[...] = _sigmoid(acc)


def _inproj(x2, sc, sh, w_in_bf, cos, slo, shi, tm, tn, seq):
    n_tok, d = x2.shape
    n_cols = w_in_bf.shape[1]
    n_slab = (tn - 2 * KV_W) // SLAB
    per_b = seq // tm
    mod_spec = pl.BlockSpec((1, 1, d), lambda i, j: (i // per_b, 0, 0))
    tab_spec = pl.BlockSpec((tm, HEAD_DIM), lambda i, j: (i % per_b, 0))
    return pl.pallas_call(
        functools.partial(_inproj_kernel, tn=tn),
        grid=(n_tok // tm, n_cols // tn),
        in_specs=[pl.BlockSpec((tm, d), lambda i, j: (i, 0)), mod_spec, mod_spec,
                  pl.BlockSpec((d, tn), lambda i, j: (0, j)), tab_spec, tab_spec, tab_spec],
        out_specs=[pl.BlockSpec((tm, tn), lambda i, j: (i, j)),
                   pl.BlockSpec((n_slab, tm // S5_CHUNK, SLAB_W), lambda i, j: (0, i % per_b, i // per_b))],
        out_shape=[jax.ShapeDtypeStruct((n_tok, n_cols), F32),
                   jax.ShapeDtypeStruct((n_slab, seq // S5_CHUNK, (n_tok // seq) * SLAB_W), F32)],
        scratch_shapes=[pltpu.VMEM((tm, d), BF16), pltpu.VMEM((n_slab, tm, SLAB), F32)],
        compiler_params=_params("arbitrary", "arbitrary"),
        name="inproj",
    )(x2, sc, sh, w_in_bf, cos, slo, shi)


def _ctxproj_kernel(x_ref, sc_ref, sh_ref, w_ref, o_ref, us_ref, u_ref):
    h = _layer_norm(x_ref[...]) * (1.0 + sc_ref[...]) + sh_ref[...]
    acc = jnp.dot(h.astype(BF16), w_ref[...], preferred_element_type=F32)
    o_ref[...] = acc
    for s in range(u_ref.shape[0]):
        u_ref[s] = acc[:, 2 * KV_W + s * SLAB:2 * KV_W + (s + 1) * SLAB]
    _rows_to_slab(u_ref, us_ref)


def _ctxproj(ctx2, sc, sh, w_in_bf, tm, tn):
    n_tok, d = ctx2.shape
    n_slab = (tn - 2 * KV_W) // SLAB
    vec = pl.BlockSpec((1, d), lambda i: (0, 0))
    return pl.pallas_call(
        _ctxproj_kernel,
        grid=(n_tok // tm,),
        in_specs=[pl.BlockSpec((tm, d), lambda i: (i, 0)), vec, vec,
                  pl.BlockSpec((d, tn), lambda i: (0, Q_W // tn))],
        out_specs=[pl.BlockSpec((tm, tn), lambda i: (i, 0)),
                   pl.BlockSpec((n_slab, tm // S5_CHUNK, SLAB_W), lambda i: (0, 0, i))],
        out_shape=[jax.ShapeDtypeStruct((n_tok, tn), F32),
                   jax.ShapeDtypeStruct((n_slab, tm // S5_CHUNK, (n_tok // tm) * SLAB_W), F32)],
        scratch_shapes=[pltpu.VMEM((n_slab, tm, SLAB), F32)],
        compiler_params=_params("arbitrary"),
        name="ctxproj",
    )(ctx2, sc, sh, w_in_bf)


def _attn_kernel(sink_ref, q_ref, k_ref, v_ref, kc_ref, vc_ref, o_ref, *, seq):
    n = pl.program_id(1)
    start = pl.multiple_of(jnp.clip((n - 1) * BLOCK, 0, seq - BAND), BLOCK)
    kb = k_ref[pl.ds(start, BAND), :].astype(BF16)
    vb = v_ref[pl.ds(start, BAND), :].astype(BF16)
    kc = kc_ref[...].astype(BF16)
    vc = vc_ref[...].astype(BF16)
    rows = GROUP * BLOCK
    q_pos = n * BLOCK + lax.broadcasted_iota(jnp.int32, (rows, BAND), 0) % BLOCK
    k_pos = start + lax.broadcasted_iota(jnp.int32, (rows, BAND), 1)
    valid = jnp.abs(q_pos - k_pos) <= WINDOW
    for kv in range(N_KV_HEADS):
        ksl = slice(kv * HEAD_DIM, (kv + 1) * HEAD_DIM)
        heads = [kv * GROUP + g for g in range(GROUP)]
        q = jnp.concatenate([q_ref[:, h * HEAD_DIM:(h + 1) * HEAD_DIM] for h in heads], 0).astype(BF16)
        s_loc = lax.dot_general(q, kb[:, ksl], NT_DIMS, preferred_element_type=F32)
        s_loc = jnp.where(valid, s_loc, NEG_INF)
        s_ctx = lax.dot_general(q, kc[:, ksl], NT_DIMS, preferred_element_type=F32)
        sink = jnp.concatenate([jnp.full((BLOCK, 1), sink_ref[h], F32) for h in heads], 0)
        m = jnp.maximum(jnp.maximum(jnp.max(s_loc, -1, keepdims=True), jnp.max(s_ctx, -1, keepdims=True)), sink)
        p_loc = jnp.exp(s_loc - m)
        p_ctx = jnp.exp(s_ctx - m)
        den = jnp.sum(p_loc, -1, keepdims=True) + jnp.sum(p_ctx, -1, keepdims=True) + jnp.exp(sink - m)
        o = (jnp.dot(p_loc.astype(BF16), vb[:, ksl], preferred_element_type=F32)
             + jnp.dot(p_ctx.astype(BF16), vc[:, ksl], preferred_element_type=F32)) / den
        for g, h in enumerate(heads):
            o_ref[:, h * HEAD_DIM:(h + 1) * HEAD_DIM] = o[g * BLOCK:(g + 1) * BLOCK].astype(o_ref.dtype)


def _attn(sink, proj, cproj, bsz, seq, n_ctx_tok):
    nb = seq // BLOCK
    kcol = Q_W // KV_W
    return pl.pallas_call(
        functools.partial(_attn_kernel, seq=seq),
        grid_spec=pltpu.PrefetchScalarGridSpec(
            num_scalar_prefetch=1,
            grid=(bsz, nb),
            in_specs=[pl.BlockSpec((BLOCK, Q_W), lambda b, n, s: (b * nb + n, 0)),
                      pl.BlockSpec((seq, KV_W), lambda b, n, s: (b, kcol)),
                      pl.BlockSpec((seq, KV_W), lambda b, n, s: (b, kcol + 1)),
                      pl.BlockSpec((n_ctx_tok, KV_W), lambda b, n, s: (b, 0)),
                      pl.BlockSpec((n_ctx_tok, KV_W), lambda b, n, s: (b, 1))],
            out_specs=pl.BlockSpec((BLOCK, Q_W), lambda b, n, s: (b * nb + n, 0)),
        ),
        out_shape=jax.ShapeDtypeStruct((bsz * seq, Q_W), BF16),
        compiler_params=_params("arbitrary", "arbitrary"),
        name="attn",
    )(sink, proj, proj, proj, cproj, cproj)


def _gelu_tanh(x):
    return 0.5 * x * (1.0 + jnp.tanh(0.7978845608028654 * (x + 0.044715 * (x * x * x))))


def _mix_kernel(a_ref, ys_ref, ga_ref, gs_ref, wg_ref, wa_ref, ws_ref, o_ref, y_ref, *, ssm_w):
    _slab_to_rows(ys_ref, y_ref)
    y = jnp.concatenate([y_ref[s] for s in range(y_ref.shape[0])], 1)
    z = jnp.dot(_gelu_tanh(y).astype(BF16), wg_ref[...], preferred_element_type=F32)
    glu = (z[:, :ssm_w] * _sigmoid(z[:, ssm_w:])).astype(BF16)
    attn_d = jnp.dot(a_ref[...], wa_ref[...], preferred_element_type=F32)
    ssm_d = jnp.dot(glu, ws_ref[...], preferred_element_type=F32)
    o_ref[...] = (ga_ref[...] * attn_d + gs_ref[...] * ssm_d).astype(o_ref.dtype)


def _mix(attn, y_slabs, proj, w_glu_bf, w_au_bf, w_su_bf, tm, d, seq):
    n_tok = attn.shape[0]
    n_slab = y_slabs.shape[0]
    ssm_w = n_slab * SLAB
    per_b = seq // tm
    ga_col = (Q_W + 2 * KV_W + ssm_w) // d
    full = lambda a: pl.BlockSpec(a.shape, lambda i: (0, 0))
    return pl.pallas_call(
        functools.partial(_mix_kernel, ssm_w=ssm_w),
        grid=(n_tok // tm,),
        in_specs=[pl.BlockSpec((tm, Q_W), lambda i: (i, 0)),
                  pl.BlockSpec((n_slab, tm // S5_CHUNK, SLAB_W), lambda i: (0, i % per_b, i // per_b)),
                  pl.BlockSpec((tm, d), lambda i: (i, ga_col)),
                  pl.BlockSpec((tm, d), lambda i: (i, ga_col + 1)),
                  full(w_glu_bf), full(w_au_bf), full(w_su_bf)],
        out_specs=pl.BlockSpec((tm, d), lambda i: (i, 0)),
        out_shape=jax.ShapeDtypeStruct((n_tok, d), BF16),
        scratch_shapes=[pltpu.VMEM((n_slab, tm, SLAB), F32)],
        compiler_params=_params("arbitrary"),
        name="mix",
    )(attn, y_slabs, proj, proj, w_glu_bf, w_au_bf, w_su_bf)


def _outproj_kernel(m_ref, x_ref, w_ref, g1_ref, lg_ref, lb_ref, sc_ref, sh_ref, x1_ref, h2_ref, *, alpha):
    y = jnp.dot(m_ref[...], w_ref[...], preferred_element_type=F32)
    x1 = _layer_norm(alpha * x_ref[...] + g1_ref[0] * y) * lg_ref[...] + lb_ref[...]
    x1_ref[...] = x1
    h2_ref[...] = (_layer_norm(x1) * (1.0 + sc_ref[0]) + sh_ref[0]).astype(h2_ref.dtype)


def _outproj(mixed, x2, w_out_bf, g1, ln_g, ln_b, sc2, sh2, tm, seq, alpha):
    n_tok, d = x2.shape
    per_b = seq // tm
    rows = pl.BlockSpec((tm, d), lambda i: (i, 0))
    mod_spec = pl.BlockSpec((1, 1, d), lambda i: (i // per_b, 0, 0))
    vec = pl.BlockSpec((1, d), lambda i: (0, 0))
    return pl.pallas_call(
        functools.partial(_outproj_kernel, alpha=alpha),
        grid=(n_tok // tm,),
        in_specs=[rows, rows, pl.BlockSpec((d, d), lambda i: (0, 0)), mod_spec, vec, vec, mod_spec, mod_spec],
        out_specs=[rows, rows],
        out_shape=[jax.ShapeDtypeStruct((n_tok, d), F32), jax.ShapeDtypeStruct((n_tok, d), BF16)],
        compiler_params=_params("arbitrary"),
        name="outproj",
    )(mixed, x2, w_out_bf, g1, ln_g, ln_b, sc2, sh2)


def _mlp_kernel(h_ref, x_ref, w1_ref, b1_ref, w2_ref, b2_ref, g2_ref, lg_ref, lb_ref, o_ref, acc_ref, *, alpha):
    f = pl.program_id(1)
    a = jnp.dot(h_ref[...], w1_ref[...], preferred_element_type=F32) + b1_ref[...]
    a = jnp.maximum(a, 0.0)
    part = jnp.dot((a * a).astype(BF16), w2_ref[...], preferred_element_type=F32)

    @pl.when(f == 0)
    def _():
        acc_ref[...] = part

    @pl.when(f > 0)
    def _():
        acc_ref[...] += part

    @pl.when(f == pl.num_programs(1) - 1)
    def _():
        y = acc_ref[...] + b2_ref[...]
        o_ref[...] = _layer_norm(alpha * x_ref[...] + g2_ref[0] * y) * lg_ref[...] + lb_ref[...]


def _mlp(h2, x1, w1_bf, b1, w2_bf, b2, g2, ln_g, ln_b, tm, tf, seq, alpha):
    n_tok, d = x1.shape
    d_ff = w1_bf.shape[1]
    per_b = seq // tm
    rows = pl.BlockSpec((tm, d), lambda i, f: (i, 0))
    vec = pl.BlockSpec((1, d), lambda i, f: (0, 0))
    return pl.pallas_call(
        functools.partial(_mlp_kernel, alpha=alpha),
        grid=(n_tok // tm, d_ff // tf),
        in_specs=[rows, rows,
                  pl.BlockSpec((d, tf), lambda i, f: (0, f)),
                  pl.BlockSpec((1, tf), lambda i, f: (0, f)),
                  pl.BlockSpec((tf, d), lambda i, f: (f, 0)),
                  vec, pl.BlockSpec((1, 1, d), lambda i, f: (i // per_b, 0, 0)), vec, vec],
        out_specs=rows,
        out_shape=jax.ShapeDtypeStruct((n_tok, d), F32),
        scratch_shapes=[pltpu.VMEM((tm, d), F32)],
        compiler_params=_params("arbitrary", "arbitrary"),
        name="mlp",
    )(h2, x1, w1_bf, b1, w2_bf, b2, g2, ln_g, ln_b)


def _rope_tables(seq):
    rows = seq // GRID_W
    row = jnp.repeat(jnp.arange(rows), GRID_W)
    col = jnp.tile(jnp.arange(GRID_W), rows)
    n_freq = ROT_HALF // 2
    freqs = ROPE_BASE ** (-jnp.arange(n_freq, dtype=F32) / n_freq)
    ang_r = row.astype(F32)[:, None] * freqs
    ang_c = col.astype(F32)[:, None] * freqs
    ang = jnp.concatenate([ang_r, ang_r, ang_c, ang_c], -1)
    cos, sin = jnp.cos(ang), jnp.sin(ang)
    low = (jnp.arange(HEAD_DIM) % ROT_HALF) < n_freq
    return cos, jnp.where(low, -sin, 0.0), jnp.where(low, 0.0, sin)


def _tiles(seq):
    return dict(mod_tn=1024, proj_tm=min(1024, seq), proj_tn=1024, row_tm=min(512, seq), mlp_tf=1024, slab_tr=128)


def kernel(x, c, ctx, c_ctx, w_ada, b_ada, w_in, attn_sink, ssm_a_re, ssm_a_im, ssm_log_dt, ssm_b_re, ssm_b_im,
           ssm_c_re, ssm_c_im, ssm_d, w_glu, w_attn_up, w_ssm_up, w_out, ln_mix_g, ln_mix_b, w_mlp1, b_mlp1,
           w_mlp2, b_mlp2, ln_mlp_g, ln_mlp_b):
    depth = w_ada.shape[0]
    assert depth == 1, "single-layer problem: the context stream is never updated"
    bsz, seq, d = x.shape
    n_ctx_tok = ctx.shape[1]
    groups = ssm_a_re.shape[2]
    ssm_w = groups * SSM_GROUP
    assert w_in.shape[2] == Q_W + 2 * KV_W + ssm_w + 2 * d and Q_W + 2 * KV_W + ssm_w == d
    assert seq % BLOCK == 0 and seq >= BAND and n_ctx_tok % S5_CHUNK == 0 and bsz == 8
    alpha = (2.0 * depth) ** 0.25
    t = _tiles(seq)

    pad = jnp.zeros((16 - bsz - 1, d), F32)
    mod = _mod(jnp.concatenate([c, c_ctx[None], pad], 0), w_ada[0], b_ada[0][None], t["mod_tn"])
    sh1, sc1, g1, sh2, sc2, g2 = [mod[:bsz, i * d:(i + 1) * d][:, None, :] for i in range(6)]
    csh1, csc1 = mod[bsz:bsz + 1, :d], mod[bsz:bsz + 1, d:2 * d]

    w_in_bf = w_in[0].astype(BF16)
    x2 = x.reshape(bsz * seq, d)
    cos, slo, shi = _rope_tables(seq)
    proj, u_lat = _inproj(x2, sc1, sh1, w_in_bf, cos, slo, shi, t["proj_tm"], t["proj_tn"], seq)
    cproj, u_ctx = _ctxproj(ctx.reshape(bsz * n_ctx_tok, d), csc1, csh1, w_in_bf, n_ctx_tok, t["proj_tn"])

    attn = _attn(attn_sink[0], proj, cproj, bsz, seq, n_ctx_tok)

    m, e, f, l16 = _s5prep(ssm_a_re[0], ssm_a_im[0], ssm_log_dt[0], ssm_b_re[0], ssm_b_im[0],
                           ssm_c_re[0], ssm_c_im[0], ssm_d[0])
    n_slab = ssm_w // SLAB
    ut_ctx = _split(u_ctx.reshape(n_slab, -1, SLAB_W), t["slab_tr"])
    ut_lat = _split(u_lat.reshape(n_slab, -1, SLAB_W), t["slab_tr"])
    y_slabs = _merge(_s5(ut_ctx, ut_lat, m, e, f, l16, bsz), t["slab_tr"])
    y_slabs = y_slabs.reshape(n_slab, seq // S5_CHUNK, bsz * SLAB_W)

    mixed = _mix(attn, y_slabs, proj, w_glu[0].astype(BF16), w_attn_up[0].astype(BF16), w_ssm_up[0].astype(BF16),
                 t["row_tm"], d, seq)
    x1, h2 = _outproj(mixed, x2, w_out[0].astype(BF16), g1, ln_mix_g, ln_mix_b, sc2, sh2, t["row_tm"], seq, alpha)
    out = _mlp(h2, x1, w_mlp1[0].astype(BF16), b_mlp1, w_mlp2[0].astype(BF16), b_mlp2, g2, ln_mlp_g, ln_mlp_b,
               t["row_tm"], t["mlp_tf"], seq, alpha)
    return out.reshape(bsz, seq, d)
```

```python
import functools

import jax
import jax.numpy as jnp
from jax import lax
from jax.experimental import pallas as pl
from jax.experimental.pallas import tpu as pltpu

F32 = jnp.float32
BF16 = jnp.bfloat16

GRID_W = 64
HEAD_DIM = 128
N_HEADS = 8
N_KV_HEADS = 2
GROUP = N_HEADS // N_KV_HEADS
WINDOW = 128
BLOCK = 128
BAND = 3 * BLOCK
ROT_HALF = HEAD_DIM // 2
ROPE_BASE = 10000.0
SSM_GROUP = 16
SSM_STATE = 64
LN_EPS = 1e-6
NEG_INF = -1e30
Q_W = N_HEADS * HEAD_DIM
KV_W = N_KV_HEADS * HEAD_DIM

S5_CHUNK = 16
S5_CW = S5_CHUNK * SSM_GROUP
S5_SW = 4 * SSM_STATE

V7X_VMEM_LIMIT_BYTES = 56 * 1024 * 1024
HI = lax.Precision.HIGHEST
NT_DIMS = (((1,), (1,)), ((), ()))


def _params(*semantics):
    return pltpu.CompilerParams(dimension_semantics=semantics, vmem_limit_bytes=V7X_VMEM_LIMIT_BYTES)


def _layer_norm(x):
    mu = jnp.mean(x, -1, keepdims=True)
    xc = x - mu
    var = jnp.mean(xc * xc, -1, keepdims=True)
    return xc * lax.rsqrt(var + LN_EPS)


def _sigmoid(x):
    return 1.0 / (1.0 + jnp.exp(-x))


def _mod_kernel(c_ref, w_ref, b_ref, o_ref):
    c = c_ref[...]
    s = (c * _sigmoid(c)).astype(BF16)
    o_ref[...] = jnp.dot(s, w_ref[...].astype(BF16), preferred_element_type=F32) + b_ref[...]


def _mod(cc, w_ada, b_ada, tn):
    rows, d = cc.shape
    n = w_ada.shape[1]
    return pl.pallas_call(
        _mod_kernel,
        grid=(n // tn,),
        in_specs=[pl.BlockSpec((rows, d), lambda j: (0, 0)),
                  pl.BlockSpec((d, tn), lambda j: (0, j)),
                  pl.BlockSpec((1, tn), lambda j: (0, j))],
        out_specs=pl.BlockSpec((rows, tn), lambda j: (0, j)),
        out_shape=jax.ShapeDtypeStruct((rows, n), F32),
        compiler_params=_params("arbitrary"),
        name="mod",
    )(cc, w_ada, b_ada)


def _s5prep_kernel(are_ref, aim_ref, ldt_ref, br_ref, bi_ref, cr_ref, ci_ref, d_ref,
                   m_ref, e_ref, f_ref, l_ref):
    ar = are_ref[0]
    ai = aim_ref[0]
    dt = jnp.exp(ldt_ref[0])
    dar = dt * ar
    dai = dt * ai
    lane = lax.broadcasted_iota(jnp.int32, (S5_CHUNK, 128), 1)
    jj = lax.broadcasted_iota(jnp.int32, (S5_CHUNK, 128), 0)
    fwd = lane < SSM_STATE

    def powers(k):
        kf = k.astype(F32)
        mag = jnp.exp(kf * dar)
        ang = kf * dai
        return mag * jnp.cos(ang), mag * jnp.sin(ang)

    l1r, l1i = powers(jnp.ones((1, 128), jnp.int32))
    x = l1r - 1.0
    den = ar * ar + ai * ai
    cfr = (x * ar + l1i * ai) / den
    cfi = (l1i * ar - x * ai) / den
    br = br_ref[0]
    bi = bi_ref[0]
    bbr = cfr * br - cfi * bi
    bbi = cfr * bi + cfi * br
    cr = cr_ref[0]
    ci = ci_ref[0]

    def expand(tab):
        return jnp.concatenate(
            [jnp.broadcast_to(tab[j:j + 1], (SSM_GROUP, 128)) for j in range(S5_CHUNK)], 0)

    def tile(xm):
        return jnp.concatenate([xm] * S5_CHUNK, 0)

    def cprod(k, xr, xi):
        tr, ti = powers(k)
        tr, ti = expand(tr), expand(ti)
        xr, xi = tile(xr), tile(xi)
        return tr * xr - ti * xi, tr * xi + ti * xr

    last = S5_CHUNK - 1
    er, ei = cprod(jnp.where(fwd, last - jj, jj), bbr, bbi)
    e_ref[0] = jnp.concatenate([er, ei], 1)
    fr, fi = cprod(jnp.where(fwd, jj + 1, S5_CHUNK - jj), cr, ci)
    f_ref[0] = jnp.concatenate([fr, -fi], 1)
    lr, li = powers(jnp.full((1, 128), S5_CHUNK, jnp.int32))
    l_ref[0] = jnp.concatenate([lr, li], 1)

    rr, ri = cprod(jnp.where(fwd, jj, last - jj), cr, ci)
    rt = jnp.concatenate([rr, -ri], 1)
    bb = jnp.concatenate([bbr, bbi], 1)
    fwd2 = jnp.concatenate([fwd, fwd], 1)
    kf = lax.dot_general(jnp.where(fwd2, bb, 0.0), rt, NT_DIMS, precision=HI, preferred_element_type=F32)
    kb = lax.dot_general(jnp.where(fwd2, 0.0, bb), rt, NT_DIMS, precision=HI, preferred_element_type=F32)

    lane2 = lax.broadcasted_iota(jnp.int32, (SSM_GROUP, S5_CW), 1)
    hh = lax.broadcasted_iota(jnp.int32, (SSM_GROUP, S5_CW), 0)
    dsk = d_ref[0]
    for s in range(S5_CHUNK):
        lo = SSM_GROUP * s
        hi = SSM_GROUP * (s + 1)
        kfs = kf if s == 0 else pltpu.roll(kf, lo, 1)
        kbs = kb if s == last else pltpu.roll(kb, hi, 1)
        blk = jnp.where(lane2 >= lo, kfs, 0.0) + jnp.where(lane2 < hi, kbs, 0.0)
        blk = blk + jnp.where(lane2 == lo + hh, dsk, 0.0)
        m_ref[0, lo:hi, :] = blk


def _s5prep(a_re, a_im, log_dt, b_re, b_im, c_re, c_im, d_skip):
    g = a_re.shape[1]

    def fb(t):
        return jnp.concatenate([t[0], t[1]], -1)

    are = fb(a_re[:, :, None, :])
    aim = fb(a_im[:, :, None, :])
    ldt = fb(jnp.broadcast_to(log_dt[:, :, None, None], (2, g, 1, SSM_STATE)))
    br = fb(jnp.swapaxes(b_re, -1, -2))
    bi = fb(jnp.swapaxes(b_im, -1, -2))
    cr = fb(c_re)
    ci = fb(c_im)
    dsk = jnp.tile(d_skip, (1, S5_CHUNK))[:, None, :]

    row = pl.BlockSpec((1, 1, 128), lambda i: (i, 0, 0))
    mat = pl.BlockSpec((1, SSM_GROUP, 128), lambda i: (i, 0, 0))
    sq = pl.BlockSpec((1, S5_CW, S5_SW), lambda i: (i, 0, 0))
    return pl.pallas_call(
        _s5prep_kernel,
        grid=(g,),
        in_specs=[row, row, row, mat, mat, mat, mat, pl.BlockSpec((1, 1, S5_CW), lambda i: (i, 0, 0))],
        out_specs=[sq, sq, sq, pl.BlockSpec((1, 1, S5_SW), lambda i: (i, 0, 0))],
        out_shape=[jax.ShapeDtypeStruct((g, S5_CW, S5_CW), F32),
                   jax.ShapeDtypeStruct((g, S5_CW, S5_SW), F32),
                   jax.ShapeDtypeStruct((g, S5_CW, S5_SW), F32),
                   jax.ShapeDtypeStruct((g, 1, S5_SW), F32)],
        compiler_params=_params("arbitrary"),
        name="s5prep",
    )(are, aim, ldt, br, bi, cr, ci, dsk)


def _s5_kernel(uc_ref, ul_ref, m_ref, e_ref, f_ref, l_ref, y_ref, s_ref, sf_ref, sb_ref, *, bsz, n_ctx, n_lat):
    u_lat = ul_ref[0].astype(BF16)
    e = e_ref[0].astype(BF16)
    s_ref[:n_ctx * bsz, :] = jnp.dot(uc_ref[0].astype(BF16), e, preferred_element_type=F32)
    s_ref[n_ctx * bsz:, :] = jnp.dot(u_lat, e, preferred_element_type=F32)
    half = S5_SW // 2
    lr = jnp.broadcast_to(l_ref[0][:, :half], (bsz, half))
    li = jnp.broadcast_to(l_ref[0][:, half:], (bsz, half))
    fwd = lax.broadcasted_iota(jnp.int32, (bsz, half), 1) < SSM_STATE
    n_all = n_ctx + n_lat

    def rows(c):
        return pl.ds(pl.multiple_of(c * bsz, bsz), bsz)

    def advance(cf, cb, carry):
        c_re, c_im = carry
        sf = s_ref[rows(cf), :]
        sb = s_ref[rows(cb), :]
        x_re = jnp.where(fwd, sf[:, :half], sb[:, :half])
        x_im = jnp.where(fwd, sf[:, half:], sb[:, half:])
        return lr * c_re - li * c_im + x_re, lr * c_im + li * c_re + x_im

    def ctx_step(i, carry):
        return advance(i, n_ctx - 1 - i, carry)

    def lat_step(i, carry):
        c_re, c_im = carry
        both = jnp.concatenate([c_re, c_im], 1)
        sf_ref[rows(i), :] = both
        sb_ref[rows(n_lat - 1 - i), :] = both
        return advance(n_ctx + i, n_all - 1 - i, carry)

    zero = jnp.zeros((bsz, half), F32)
    carry = lax.fori_loop(0, n_ctx, ctx_step, (zero, zero))
    lax.fori_loop(0, n_lat, lat_step, carry)

    fwd2 = (lax.broadcasted_iota(jnp.int32, (1, S5_SW), 1) % half) < SSM_STATE
    s_in = jnp.where(fwd2, sf_ref[...], sb_ref[...]).astype(BF16)
    y_ref[0] = (jnp.dot(u_lat, m_ref[0].astype(BF16), preferred_element_type=F32)
                + lax.dot_general(s_in, f_ref[0].astype(BF16), NT_DIMS, preferred_element_type=F32))


def _s5(ut_ctx, ut_lat, m, e, f, l16, bsz):
    g, rows_ctx, _ = ut_ctx.shape
    rows_lat = ut_lat.shape[1]
    sq = pl.BlockSpec((1, S5_CW, S5_SW), lambda i: (i, 0, 0))
    return pl.pallas_call(
        functools.partial(_s5_kernel, bsz=bsz, n_ctx=rows_ctx // bsz, n_lat=rows_lat // bsz),
        grid=(g,),
        in_specs=[pl.BlockSpec((1, rows_ctx, S5_CW), lambda i: (i, 0, 0)),
                  pl.BlockSpec((1, rows_lat, S5_CW), lambda i: (i, 0, 0)), sq, sq, sq,
                  pl.BlockSpec((1, 1, S5_SW), lambda i: (i, 0, 0))],
        out_specs=pl.BlockSpec((1, rows_lat, S5_CW), lambda i: (i, 0, 0)),
        out_shape=jax.ShapeDtypeStruct((g, rows_lat, S5_CW), F32),
        scratch_shapes=[pltpu.VMEM((rows_ctx + rows_lat, S5_SW), F32),
                        pltpu.VMEM((rows_lat, S5_SW), F32),
                        pltpu.VMEM((rows_lat, S5_SW), F32)],
        compiler_params=_params("arbitrary"),
        name="s5",
    )(ut_ctx, ut_lat, m, e, f, l16)


SLAB = 128
SLAB_GROUPS = SLAB // SSM_GROUP
SLAB_W = S5_CHUNK * SLAB


def _rows_to_slab(src_ref, dst_ref):
    n_slab, n_chunks, _ = dst_ref.shape
    for t in range(S5_CHUNK):
        for s in range(n_slab):
            dst_ref[s, :, t * SLAB:(t + 1) * SLAB] = src_ref[s, pl.ds(t, n_chunks, stride=S5_CHUNK), :]


def _slab_to_rows(src_ref, dst_ref):
    n_slab, n_chunks, _ = src_ref.shape
    for t in range(S5_CHUNK):
        for s in range(n_slab):
            dst_ref[s, pl.ds(t, n_chunks, stride=S5_CHUNK), :] = src_ref[s, :, t * SLAB:(t + 1) * SLAB]


def _piece_transpose(tiles):
    piece = lax.broadcasted_iota(jnp.int32, tiles[0].shape, 1) // SSM_GROUP
    tiles = list(tiles)
    k = SLAB_GROUPS // 2
    while k:
        keep_lo = (piece & k) == 0
        for a in range(SLAB_GROUPS):
            if a & k:
                continue
            lo, hi = tiles[a], tiles[a + k]
            tiles[a] = jnp.where(keep_lo, lo, pltpu.roll(hi, k * SSM_GROUP, 1))
            tiles[a + k] = jnp.where(keep_lo, pltpu.roll(lo, SLAB - k * SSM_GROUP, 1), hi)
        k //= 2
    return tiles


def _split_kernel(x_ref, o_ref):
    x = x_ref[0]
    halves = []
    for th in range(S5_CHUNK // SLAB_GROUPS):
        tiles = [x[:, (th * SLAB_GROUPS + a) * SLAB:(th * SLAB_GROUPS + a + 1) * SLAB] for a in range(SLAB_GROUPS)]
        halves.append(_piece_transpose(tiles))
    for g in range(SLAB_GROUPS):
        o_ref[g] = jnp.concatenate([h[g] for h in halves], 1)


def _split(slabs, tr):
    n_slab, rows, _ = slabs.shape
    return pl.pallas_call(
        _split_kernel,
        grid=(n_slab, rows // tr),
        in_specs=[pl.BlockSpec((1, tr, SLAB_W), lambda s, r: (s, r, 0))],
        out_specs=pl.BlockSpec((SLAB_GROUPS, tr, S5_CW), lambda s, r: (s, r, 0)),
        out_shape=jax.ShapeDtypeStruct((n_slab * SLAB_GROUPS, rows, S5_CW), F32),
        compiler_params=_params("arbitrary", "arbitrary"),
        name="split",
    )(slabs)


def _merge_kernel(y_ref, o_ref):
    for th in range(S5_CHUNK // SLAB_GROUPS):
        tiles = [y_ref[g, :, th * SLAB:(th + 1) * SLAB] for g in range(SLAB_GROUPS)]
        for a, tile in enumerate(_piece_transpose(tiles)):
            t = th * SLAB_GROUPS + a
            o_ref[0, :, t * SLAB:(t + 1) * SLAB] = tile


def _merge(y, tr):
    g, rows, _ = y.shape
    return pl.pallas_call(
        _merge_kernel,
        grid=(g // SLAB_GROUPS, rows // tr),
        in_specs=[pl.BlockSpec((SLAB_GROUPS, tr, S5_CW), lambda s, r: (s, r, 0))],
        out_specs=pl.BlockSpec((1, tr, SLAB_W), lambda s, r: (s, r, 0)),
        out_shape=jax.ShapeDtypeStruct((g // SLAB_GROUPS, rows, SLAB_W), F32),
        compiler_params=_params("arbitrary", "arbitrary"),
        name="merge",
    )(y)


def _rope(x, cos, sin_lo, sin_hi):
    return (x * cos + pltpu.roll(x, HEAD_DIM - ROT_HALF // 2, 1) * sin_lo
            + pltpu.roll(x, ROT_HALF // 2, 1) * sin_hi)


def _inproj_kernel(x_ref, sc_ref, sh_ref, w_ref, cos_ref, slo_ref, shi_ref, o_ref, us_ref, h_ref, u_ref, *, tn):
    j = pl.program_id(1)

    @pl.when(j == 0)
    def _():
        h = _layer_norm(x_ref[...]) * (1.0 + sc_ref[0]) + sh_ref[0]
        h_ref[...] = h.astype(BF16)

    acc = jnp.dot(h_ref[...], w_ref[...], preferred_element_type=F32)
    q_tiles = Q_W // tn
    scale = HEAD_DIM ** -0.5

    @pl.when(j < q_tiles)
    def _():
        cos, slo, shi = cos_ref[...], slo_ref[...], shi_ref[...]
        for hd in range(tn // HEAD_DIM):
            sl = slice(hd * HEAD_DIM, (hd + 1) * HEAD_DIM)
            o_ref[:, sl] = _rope(acc[:, sl], cos, slo, shi) * scale

    @pl.when(j == q_tiles)
    def _():
        cos, slo, shi = cos_ref[...], slo_ref[...], shi_ref[...]
        for hd in range(N_KV_HEADS):
            sl = slice(hd * HEAD_DIM, (hd + 1) * HEAD_DIM)
            o_ref[:, sl] = _rope(acc[:, sl], cos, slo, shi)
        o_ref[:, KV_W:] = acc[:, KV_W:]
        for s in range(u_ref.shape[0]):
            u_ref[s] = acc[:, 2 * KV_W + s * SLAB:2 * KV_W + (s + 1) * SLAB]
        _rows_to_slab(u_ref, us_ref)

    @pl.when(j > q_tiles)
    def _():
        o_ref[...] = _sigmoid(acc)


def _inproj(x2, sc, sh, w_in_bf, cos, slo, shi, tm, tn, seq):
    n_tok, d = x2.shape
    n_cols = w_in_bf.shape[1]
    n_slab = (tn - 2 * KV_W) // SLAB
    per_b = seq // tm
    mod_spec = pl.BlockSpec((1, 1, d), lambda i, j: (i // per_b, 0, 0))
    tab_spec = pl.BlockSpec((tm, HEAD_DIM), lambda i, j: (i % per_b, 0))
    return pl.pallas_call(
        functools.partial(_inproj_kernel, tn=tn),
        grid=(n_tok // tm, n_cols // tn),
        in_specs=[pl.BlockSpec((tm, d), lambda i, j: (i, 0)), mod_spec, mod_spec,
                  pl.BlockSpec((d, tn), lambda i, j: (0, j)), tab_spec, tab_spec, tab_spec],
        out_specs=[pl.BlockSpec((tm, tn), lambda i, j: (i, j)),
                   pl.BlockSpec((n_slab, tm // S5_CHUNK, SLAB_W), lambda i, j: (0, i % per_b, i // per_b))],
        out_shape=[jax.ShapeDtypeStruct((n_tok, n_cols), F32),
                   jax.ShapeDtypeStruct((n_slab, seq // S5_CHUNK, (n_tok // seq) * SLAB_W), F32)],
        scratch_shapes=[pltpu.VMEM((tm, d), BF16), pltpu.VMEM((n_slab, tm, SLAB), F32)],
        compiler_params=_params("arbitrary", "arbitrary"),
        name="inproj",
    )(x2, sc, sh, w_in_bf, cos, slo, shi)


def _ctxproj_kernel(x_ref, sc_ref, sh_ref, w_ref, o_ref, us_ref, u_ref):
    h = _layer_norm(x_ref[...]) * (1.0 + sc_ref[...]) + sh_ref[...]
    acc = jnp.dot(h.astype(BF16), w_ref[...], preferred_element_type=F32)
    o_ref[...] = acc
    for s in range(u_ref.shape[0]):
        u_ref[s] = acc[:, 2 * KV_W + s * SLAB:2 * KV_W + (s + 1) * SLAB]
    _rows_to_slab(u_ref, us_ref)


def _ctxproj(ctx2, sc, sh, w_in_bf, tm, tn):
    n_tok, d = ctx2.shape
    n_slab = (tn - 2 * KV_W) // SLAB
    vec = pl.BlockSpec((1, d), lambda i: (0, 0))
    return pl.pallas_call(
        _ctxproj_kernel,
        grid=(n_tok // tm,),
        in_specs=[pl.BlockSpec((tm, d), lambda i: (i, 0)), vec, vec,
                  pl.BlockSpec((d, tn), lambda i: (0, Q_W // tn))],
        out_specs=[pl.BlockSpec((tm, tn), lambda i: (i, 0)),
                   pl.BlockSpec((n_slab, tm // S5_CHUNK, SLAB_W), lambda i: (0, 0, i))],
        out_shape=[jax.ShapeDtypeStruct((n_tok, tn), F32),
                   jax.ShapeDtypeStruct((n_slab, tm // S5_CHUNK, (n_tok // tm) * SLAB_W), F32)],
        scratch_shapes=[pltpu.VMEM((n_slab, tm, SLAB), F32)],
        compiler_params=_params("arbitrary"),
        name="ctxproj",
    )(ctx2, sc, sh, w_in_bf)


def _attn_kernel(sink_ref, q_ref, k_ref, v_ref, kc_ref, vc_ref, o_ref, *, seq):
    n = pl.program_id(1)
    start = pl.multiple_of(jnp.clip((n - 1) * BLOCK, 0, seq - BAND), BLOCK)
    kb = k_ref[pl.ds(start, BAND), :].astype(BF16)
    vb = v_ref[pl.ds(start, BAND), :].astype(BF16)
    kc = kc_ref[...].astype(BF16)
    vc = vc_ref[...].astype(BF16)
    rows = GROUP * BLOCK
    q_pos = n * BLOCK + lax.broadcasted_iota(jnp.int32, (rows, BAND), 0) % BLOCK
    k_pos = start + lax.broadcasted_iota(jnp.int32, (rows, BAND), 1)
    valid = jnp.abs(q_pos - k_pos) <= WINDOW
    for kv in range(N_KV_HEADS):
        ksl = slice(kv * HEAD_DIM, (kv + 1) * HEAD_DIM)
        heads = [kv * GROUP + g for g in range(GROUP)]
        q = jnp.concatenate([q_ref[:, h * HEAD_DIM:(h + 1) * HEAD_DIM] for h in heads], 0).astype(BF16)
        s_loc = lax.dot_general(q, kb[:, ksl], NT_DIMS, preferred_element_type=F32)
        s_loc = jnp.where(valid, s_loc, NEG_INF)
        s_ctx = lax.dot_general(q, kc[:, ksl], NT_DIMS, preferred_element_type=F32)
        sink = jnp.concatenate([jnp.full((BLOCK, 1), sink_ref[h], F32) for h in heads], 0)
        m = jnp.maximum(jnp.maximum(jnp.max(s_loc, -1, keepdims=True), jnp.max(s_ctx, -1, keepdims=True)), sink)
        p_loc = jnp.exp(s_loc - m)
        p_ctx = jnp.exp(s_ctx - m)
        den = jnp.sum(p_loc, -1, keepdims=True) + jnp.sum(p_ctx, -1, keepdims=True) + jnp.exp(sink - m)
        o = (jnp.dot(p_loc.astype(BF16), vb[:, ksl], preferred_element_type=F32)
             + jnp.dot(p_ctx.astype(BF16), vc[:, ksl], preferred_element_type=F32)) / den
        for g, h in enumerate(heads):
            o_ref[:, h * HEAD_DIM:(h + 1) * HEAD_DIM] = o[g * BLOCK:(g + 1) * BLOCK].astype(o_ref.dtype)


def _attn(sink, proj, cproj, bsz, seq, n_ctx_tok):
    nb = seq // BLOCK
    kcol = Q_W // KV_W
    return pl.pallas_call(
        functools.partial(_attn_kernel, seq=seq),
        grid_spec=pltpu.PrefetchScalarGridSpec(
            num_scalar_prefetch=1,
            grid=(bsz, nb),
            in_specs=[pl.BlockSpec((BLOCK, Q_W), lambda b, n, s: (b * nb + n, 0)),
                      pl.BlockSpec((seq, KV_W), lambda b, n, s: (b, kcol)),
                      pl.BlockSpec((seq, KV_W), lambda b, n, s: (b, kcol + 1)),
                      pl.BlockSpec((n_ctx_tok, KV_W), lambda b, n, s: (b, 0)),
                      pl.BlockSpec((n_ctx_tok, KV_W), lambda b, n, s: (b, 1))],
            out_specs=pl.BlockSpec((BLOCK, Q_W), lambda b, n, s: (b * nb + n, 0)),
        ),
        out_shape=jax.ShapeDtypeStruct((bsz * seq, Q_W), BF16),
        compiler_params=_params("arbitrary", "arbitrary"),
        name="attn",
    )(sink, proj, proj, proj, cproj, cproj)


def _gelu_tanh(x):
    return 0.5 * x * (1.0 + jnp.tanh(0.7978845608028654 * (x + 0.044715 * (x * x * x))))


def _mix_kernel(a_ref, ys_ref, ga_ref, gs_ref, wg_ref, wa_ref, ws_ref, o_ref, y_ref, *, ssm_w):
    _slab_to_rows(ys_ref, y_ref)
    y = jnp.concatenate([y_ref[s] for s in range(y_ref.shape[0])], 1)
    z = jnp.dot(_gelu_tanh(y).astype(BF16), wg_ref[...], preferred_element_type=F32)
    glu = (z[:, :ssm_w] * _sigmoid(z[:, ssm_w:])).astype(BF16)
    attn_d = jnp.dot(a_ref[...], wa_ref[...], preferred_element_type=F32)
    ssm_d = jnp.dot(glu, ws_ref[...], preferred_element_type=F32)
    o_ref[...] = (ga_ref[...] * attn_d + gs_ref[...] * ssm_d).astype(o_ref.dtype)


def _mix(attn, y_slabs, proj, w_glu_bf, w_au_bf, w_su_bf, tm, d, seq):
    n_tok = attn.shape[0]
    n_slab = y_slabs.shape[0]
    ssm_w = n_slab * SLAB
    per_b = seq // tm
    ga_col = (Q_W + 2 * KV_W + ssm_w) // d
    full = lambda a: pl.BlockSpec(a.shape, lambda i: (0, 0))
    return pl.pallas_call(
        functools.partial(_mix_kernel, ssm_w=ssm_w),
        grid=(n_tok // tm,),
        in_specs=[pl.BlockSpec((tm, Q_W), lambda i: (i, 0)),
                  pl.BlockSpec((n_slab, tm // S5_CHUNK, SLAB_W), lambda i: (0, i % per_b, i // per_b)),
                  pl.BlockSpec((tm, d), lambda i: (i, ga_col)),
                  pl.BlockSpec((tm, d), lambda i: (i, ga_col + 1)),
                  full(w_glu_bf), full(w_au_bf), full(w_su_bf)],
        out_specs=pl.BlockSpec((tm, d), lambda i: (i, 0)),
        out_shape=jax.ShapeDtypeStruct((n_tok, d), BF16),
        scratch_shapes=[pltpu.VMEM((n_slab, tm, SLAB), F32)],
        compiler_params=_params("arbitrary"),
        name="mix",
    )(attn, y_slabs, proj, proj, w_glu_bf, w_au_bf, w_su_bf)


def _outproj_kernel(m_ref, x_ref, w_ref, g1_ref, lg_ref, lb_ref, sc_ref, sh_ref, x1_ref, h2_ref, *, alpha):
    y = jnp.dot(m_ref[...], w_ref[...], preferred_element_type=F32)
    x1 = _layer_norm(alpha * x_ref[...] + g1_ref[0] * y) * lg_ref[...] + lb_ref[...]
    x1_ref[...] = x1
    h2_ref[...] = (_layer_norm(x1) * (1.0 + sc_ref[0]) + sh_ref[0]).astype(h2_ref.dtype)


def _outproj(mixed, x2, w_out_bf, g1, ln_g, ln_b, sc2, sh2, tm, seq, alpha):
    n_tok, d = x2.shape
    per_b = seq // tm
    rows = pl.BlockSpec((tm, d), lambda i: (i, 0))
    mod_spec = pl.BlockSpec((1, 1, d), lambda i: (i // per_b, 0, 0))
    vec = pl.BlockSpec((1, d), lambda i: (0, 0))
    return pl.pallas_call(
        functools.partial(_outproj_kernel, alpha=alpha),
        grid=(n_tok // tm,),
        in_specs=[rows, rows, pl.BlockSpec((d, d), lambda i: (0, 0)), mod_spec, vec, vec, mod_spec, mod_spec],
        out_specs=[rows, rows],
        out_shape=[jax.ShapeDtypeStruct((n_tok, d), F32), jax.ShapeDtypeStruct((n_tok, d), BF16)],
        compiler_params=_params("arbitrary"),
        name="outproj",
    )(mixed, x2, w_out_bf, g1, ln_g, ln_b, sc2, sh2)


def _mlp_kernel(h_ref, x_ref, w1_ref, b1_ref, w2_ref, b2_ref, g2_ref, lg_ref, lb_ref, o_ref, acc_ref, *, alpha):
    f = pl.program_id(1)
    a = jnp.dot(h_ref[...], w1_ref[...], preferred_element_type=F32) + b1_ref[...]
    a = jnp.maximum(a, 0.0)
    part = jnp.dot((a * a).astype(BF16), w2_ref[...], preferred_element_type=F32)

    @pl.when(f == 0)
    def _():
        acc_ref[...] = part

    @pl.when(f > 0)
    def _():
        acc_ref[...] += part

    @pl.when(f == pl.num_programs(1) - 1)
    def _():
        y = acc_ref[...] + b2_ref[...]
        o_ref[...] = _layer_norm(alpha * x_ref[...] + g2_ref[0] * y) * lg_ref[...] + lb_ref[...]


def _mlp(h2, x1, w1_bf, b1, w2_bf, b2, g2, ln_g, ln_b, tm, tf, seq, alpha):
    n_tok, d = x1.shape
    d_ff = w1_bf.shape[1]
    per_b = seq // tm
    rows = pl.BlockSpec((tm, d), lambda i, f: (i, 0))
    vec = pl.BlockSpec((1, d), lambda i, f: (0, 0))
    return pl.pallas_call(
        functools.partial(_mlp_kernel, alpha=alpha),
        grid=(n_tok // tm, d_ff // tf),
        in_specs=[rows, rows,
                  pl.BlockSpec((d, tf), lambda i, f: (0, f)),
                  pl.BlockSpec((1, tf), lambda i, f: (0, f)),
                  pl.BlockSpec((tf, d), lambda i, f: (f, 0)),
                  vec, pl.BlockSpec((1, 1, d), lambda i, f: (i // per_b, 0, 0)), vec, vec],
        out_specs=rows,
        out_shape=jax.ShapeDtypeStruct((n_tok, d), F32),
        scratch_shapes=[pltpu.VMEM((tm, d), F32)],
        compiler_params=_params("arbitrary", "arbitrary"),
        name="mlp",
    )(h2, x1, w1_bf, b1, w2_bf, b2, g2, ln_g, ln_b)


def _rope_tables(seq):
    rows = seq // GRID_W
    row = jnp.repeat(jnp.arange(rows), GRID_W)
    col = jnp.tile(jnp.arange(GRID_W), rows)
    n_freq = ROT_HALF // 2
    freqs = ROPE_BASE ** (-jnp.arange(n_freq, dtype=F32) / n_freq)
    ang_r = row.astype(F32)[:, None] * freqs
    ang_c = col.astype(F32)[:, None] * freqs
    ang = jnp.concatenate([ang_r, ang_r, ang_c, ang_c], -1)
    cos, sin = jnp.cos(ang), jnp.sin(ang)
    low = (jnp.arange(HEAD_DIM) % ROT_HALF) < n_freq
    return cos, jnp.where(low, -sin, 0.0), jnp.where(low, 0.0, sin)


def _tiles(seq):
    return dict(mod_tn=1024, proj_tm=min(1024, seq), proj_tn=1024, row_tm=min(512, seq), mlp_tf=1024, slab_tr=128)


def kernel(x, c, ctx, c_ctx, w_ada, b_ada, w_in, attn_sink, ssm_a_re, ssm_a_im, ssm_log_dt, ssm_b_re, ssm_b_im,
           ssm_c_re, ssm_c_im, ssm_d, w_glu, w_attn_up, w_ssm_up, w_out, ln_mix_g, ln_mix_b, w_mlp1, b_mlp1,
           w_mlp2, b_mlp2, ln_mlp_g, ln_mlp_b):
    depth = w_ada.shape[0]
    assert depth == 1, "single-layer problem: the context stream is never updated"
    bsz, seq, d = x.shape
    n_ctx_tok = ctx.shape[1]
    groups = ssm_a_re.shape[2]
    ssm_w = groups * SSM_GROUP
    assert w_in.shape[2] == Q_W + 2 * KV_W + ssm_w + 2 * d and Q_W + 2 * KV_W + ssm_w == d
    assert seq % BLOCK == 0 and seq >= BAND and n_ctx_tok % S5_CHUNK == 0 and bsz == 8
    alpha = (2.0 * depth) ** 0.25
    t = _tiles(seq)

    pad = jnp.zeros((16 - bsz - 1, d), F32)
    mod = _mod(jnp.concatenate([c, c_ctx[None], pad], 0), w_ada[0], b_ada[0][None], t["mod_tn"])
    sh1, sc1, g1, sh2, sc2, g2 = [mod[:bsz, i * d:(i + 1) * d][:, None, :] for i in range(6)]
    csh1, csc1 = mod[bsz:bsz + 1, :d], mod[bsz:bsz + 1, d:2 * d]

    w_in_bf = w_in[0].astype(BF16)
    x2 = x.reshape(bsz * seq, d)
    cos, slo, shi = _rope_tables(seq)
    proj, u_lat = _inproj(x2, sc1, sh1, w_in_bf, cos, slo, shi, t["proj_tm"], t["proj_tn"], seq)
    cproj, u_ctx = _ctxproj(ctx.reshape(bsz * n_ctx_tok, d), csc1, csh1, w_in_bf, n_ctx_tok, t["proj_tn"])

    attn = _attn(attn_sink[0], proj, cproj, bsz, seq, n_ctx_tok)

    m, e, f, l16 = _s5prep(ssm_a_re[0], ssm_a_im[0], ssm_log_dt[0], ssm_b_re[0], ssm_b_im[0],
                           ssm_c_re[0], ssm_c_im[0], ssm_d[0])
    n_slab = ssm_w // SLAB
    ut_ctx = _split(u_ctx.reshape(n_slab, -1, SLAB_W), t["slab_tr"])
    ut_lat = _split(u_lat.reshape(n_slab, -1, SLAB_W), t["slab_tr"])
    y_slabs = _merge(_s5(ut_ctx, ut_lat, m, e, f, l16, bsz), t["slab_tr"])
    y_slabs = y_slabs.reshape(n_slab, seq // S5_CHUNK, bsz * SLAB_W)

    mixed = _mix(attn, y_slabs, proj, w_glu[0].astype(BF16), w_attn_up[0].astype(BF16), w_ssm_up[0].astype(BF16),
                 t["row_tm"], d, seq)
    x1, h2 = _outproj(mixed, x2, w_out[0].astype(BF16), g1, ln_mix_g, ln_mix_b, sc2, sh2, t["row_tm"], seq, alpha)
    out = _mlp(h2, x1, w_mlp1[0].astype(BF16), b_mlp1, w_mlp2[0].astype(BF16), b_mlp2, g2, ln_mlp_g, ln_mlp_b,
               t["row_tm"], t["mlp_tf"], seq, alpha)
    return out.reshape(bsz, seq, d)
```

```python
import functools

import jax
import jax.numpy as jnp
from jax import lax
from jax.experimental import pallas as pl
from jax.experimental.pallas import tpu as pltpu

F32 = jnp.float32
BF16 = jnp.bfloat16

GRID_W = 64
HEAD_DIM = 128
N_HEADS = 8
N_KV_HEADS = 2
GROUP = N_HEADS // N_KV_HEADS
WINDOW = 128
BLOCK = 128
BAND = 3 * BLOCK
ROT_HALF = HEAD_DIM // 2
ROPE_BASE = 10000.0
SSM_GROUP = 16
SSM_STATE = 64
LN_EPS = 1e-6
NEG_INF = -1e30
Q_W = N_HEADS * HEAD_DIM
KV_W = N_KV_HEADS * HEAD_DIM

S5_CHUNK = 16
S5_CW = S5_CHUNK * SSM_GROUP
S5_SW = 4 * SSM_STATE
SLAB = 128
SLAB_GROUPS = SLAB // SSM_GROUP
SLAB_W = S5_CHUNK * SLAB
HALVES = S5_CW // SLAB

V7X_VMEM_LIMIT_BYTES = 56 * 1024 * 1024
HI = lax.Precision.HIGHEST
NT_DIMS = (((1,), (1,)), ((), ()))


def _params(*semantics):
    return pltpu.CompilerParams(dimension_semantics=semantics, vmem_limit_bytes=V7X_VMEM_LIMIT_BYTES)


def _layer_norm(x):
    mu = jnp.mean(x, -1, keepdims=True)
    xc = x - mu
    var = jnp.mean(xc * xc, -1, keepdims=True)
    return xc * lax.rsqrt(var + LN_EPS)


def _sigmoid(x):
    return 1.0 / (1.0 + jnp.exp(-x))


def _mod_kernel(c_ref, w_ref, b_ref, o_ref):
    c = c_ref[...]
    s = (c * _sigmoid(c)).astype(BF16)
    o_ref[...] = jnp.dot(s, w_ref[...].astype(BF16), preferred_element_type=F32) + b_ref[...]


def _mod(cc, w_ada, b_ada, tn):
    rows, d = cc.shape
    n = w_ada.shape[1]
    return pl.pallas_call(
        _mod_kernel,
        grid=(n // tn,),
        in_specs=[pl.BlockSpec((rows, d), lambda j: (0, 0)),
                  pl.BlockSpec((d, tn), lambda j: (0, j)),
                  pl.BlockSpec((1, tn), lambda j: (0, j))],
        out_specs=pl.BlockSpec((rows, tn), lambda j: (0, j)),
        out_shape=jax.ShapeDtypeStruct((rows, n), F32),
        compiler_params=_params("arbitrary"),
        name="mod",
    )(cc, w_ada, b_ada)


def _s5prep_kernel(are_ref, aim_ref, ldt_ref, br_ref, bi_ref, cr_ref, ci_ref, d_ref,
                   m_ref, e_ref, f_ref, l_ref):
    ar = are_ref[0]
    ai = aim_ref[0]
    dt = jnp.exp(ldt_ref[0])
    dar = dt * ar
    dai = dt * ai
    lane = lax.broadcasted_iota(jnp.int32, (S5_CHUNK, 128), 1)
    jj = lax.broadcasted_iota(jnp.int32, (S5_CHUNK, 128), 0)
    fwd = lane < SSM_STATE

    def powers(k):
        kf = k.astype(F32)
        mag = jnp.exp(kf * dar)
        ang = kf * dai
        return mag * jnp.cos(ang), mag * jnp.sin(ang)

    l1r, l1i = powers(jnp.ones((1, 128), jnp.int32))
    x = l1r - 1.0
    den = ar * ar + ai * ai
    cfr = (x * ar + l1i * ai) / den
    cfi = (l1i * ar - x * ai) / den
    br = br_ref[0]
    bi = bi_ref[0]
    bbr = cfr * br - cfi * bi
    bbi = cfr * bi + cfi * br
    cr = cr_ref[0]
    ci = ci_ref[0]

    def expand(tab):
        return jnp.concatenate(
            [jnp.broadcast_to(tab[j:j + 1], (SSM_GROUP, 128)) for j in range(S5_CHUNK)], 0)

    def tile(xm):
        return jnp.concatenate([xm] * S5_CHUNK, 0)

    def cprod(k, xr, xi):
        tr, ti = powers(k)
        tr, ti = expand(tr), expand(ti)
        xr, xi = tile(xr), tile(xi)
        return tr * xr - ti * xi, tr * xi + ti * xr

    last = S5_CHUNK - 1
    er, ei = cprod(jnp.where(fwd, last - jj, jj), bbr, bbi)
    e_ref[0] = jnp.concatenate([er, ei], 1)
    fr, fi = cprod(jnp.where(fwd, jj + 1, S5_CHUNK - jj), cr, ci)
    f_ref[0] = jnp.concatenate([fr, -fi], 1)
    lr, li = powers(jnp.full((1, 128), S5_CHUNK, jnp.int32))
    l_ref[0] = jnp.concatenate([lr, li], 1)

    rr, ri = cprod(jnp.where(fwd, jj, last - jj), cr, ci)
    rt = jnp.concatenate([rr, -ri], 1)
    bb = jnp.concatenate([bbr, bbi], 1)
    fwd2 = jnp.concatenate([fwd, fwd], 1)
    kf = lax.dot_general(jnp.where(fwd2, bb, 0.0), rt, NT_DIMS, precision=HI, preferred_element_type=F32)
    kb = lax.dot_general(jnp.where(fwd2, 0.0, bb), rt, NT_DIMS, precision=HI, preferred_element_type=F32)

    lane2 = lax.broadcasted_iota(jnp.int32, (SSM_GROUP, S5_CW), 1)
    hh = lax.broadcasted_iota(jnp.int32, (SSM_GROUP, S5_CW), 0)
    dsk = d_ref[0]
    for s in range(S5_CHUNK):
        lo = SSM_GROUP * s
        hi = SSM_GROUP * (s + 1)
        kfs = kf if s == 0 else pltpu.roll(kf, lo, 1)
        kbs = kb if s == last else pltpu.roll(kb, hi, 1)
        blk = jnp.where(lane2 >= lo, kfs, 0.0) + jnp.where(lane2 < hi, kbs, 0.0)
        blk = blk + jnp.where(lane2 == lo + hh, dsk, 0.0)
        m_ref[0, lo:hi, :] = blk


def _s5prep(a_re, a_im, log_dt, b_re, b_im, c_re, c_im, d_skip):
    g = a_re.shape[1]

    def fb(t):
        return jnp.concatenate([t[0], t[1]], -1)

    are = fb(a_re[:, :, None, :])
    aim = fb(a_im[:, :, None, :])
    ldt = fb(jnp.broadcast_to(log_dt[:, :, None, None], (2, g, 1, SSM_STATE)))
    br = fb(jnp.swapaxes(b_re, -1, -2))
    bi = fb(jnp.swapaxes(b_im, -1, -2))
    cr = fb(c_re)
    ci = fb(c_im)
    dsk = jnp.tile(d_skip, (1, S5_CHUNK))[:, None, :]

    row = pl.BlockSpec((1, 1, 128), lambda i: (i, 0, 0))
    mat = pl.BlockSpec((1, SSM_GROUP, 128), lambda i: (i, 0, 0))
    sq = pl.BlockSpec((1, S5_CW, S5_SW), lambda i: (i, 0, 0))
    return pl.pallas_call(
        _s5prep_kernel,
        grid=(g,),
        in_specs=[row, row, row, mat, mat, mat, mat, pl.BlockSpec((1, 1, S5_CW), lambda i: (i, 0, 0))],
        out_specs=[sq, sq, sq, pl.BlockSpec((1, 1, S5_SW), lambda i: (i, 0, 0))],
        out_shape=[jax.ShapeDtypeStruct((g, S5_CW, S5_CW), F32),
                   jax.ShapeDtypeStruct((g, S5_CW, S5_SW), F32),
                   jax.ShapeDtypeStruct((g, S5_CW, S5_SW), F32),
                   jax.ShapeDtypeStruct((g, 1, S5_SW), F32)],
        compiler_params=_params("arbitrary"),
        name="s5prep",
    )(are, aim, ldt, br, bi, cr, ci, dsk)


def _s5_kernel(uc_ref, ul_ref, m_ref, e_ref, f_ref, l_ref, y_ref, s_ref, sf_ref, sb_ref, *, bsz, n_ctx, n_lat):
    u_lat = jnp.concatenate([ul_ref[h] for h in range(HALVES)], 1).astype(BF16)
    u_ctx = jnp.concatenate([uc_ref[h] for h in range(HALVES)], 1).astype(BF16)
    e = e_ref[0].astype(BF16)
    s_ref[:n_ctx * bsz, :] = jnp.dot(u_ctx, e, preferred_element_type=F32)
    s_ref[n_ctx * bsz:, :] = jnp.dot(u_lat, e, preferred_element_type=F32)
    half = S5_SW // 2
    lr = jnp.broadcast_to(l_ref[0][:, :half], (bsz, half))
    li = jnp.broadcast_to(l_ref[0][:, half:], (bsz, half))
    fwd = lax.broadcasted_iota(jnp.int32, (bsz, half), 1) < SSM_STATE
    n_all = n_ctx + n_lat

    def rows(c):
        return pl.ds(pl.multiple_of(c * bsz, bsz), bsz)

    def advance(cf, cb, carry):
        c_re, c_im = carry
        sf = s_ref[rows(cf), :]
        sb = s_ref[rows(cb), :]
        x_re = jnp.where(fwd, sf[:, :half], sb[:, :half])
        x_im = jnp.where(fwd, sf[:, half:], sb[:, half:])
        return lr * c_re - li * c_im + x_re, lr * c_im + li * c_re + x_im

    def ctx_step(i, carry):
        return advance(i, n_ctx - 1 - i, carry)

    def lat_step(i, carry):
        c_re, c_im = carry
        both = jnp.concatenate([c_re, c_im], 1)
        sf_ref[rows(i), :] = both
        sb_ref[rows(n_lat - 1 - i), :] = both
        return advance(n_ctx + i, n_all - 1 - i, carry)

    zero = jnp.zeros((bsz, half), F32)
    carry = lax.fori_loop(0, n_ctx, ctx_step, (zero, zero))
    lax.fori_loop(0, n_lat, lat_step, carry)

    fwd2 = (lax.broadcasted_iota(jnp.int32, (1, S5_SW), 1) % half) < SSM_STATE
    s_in = jnp.where(fwd2, sf_ref[...], sb_ref[...]).astype(BF16)
    y = (jnp.dot(u_lat, m_ref[0].astype(BF16), preferred_element_type=F32)
         + lax.dot_general(s_in, f_ref[0].astype(BF16), NT_DIMS, preferred_element_type=F32))
    for h in range(HALVES):
        y_ref[h] = y[:, h * SLAB:(h + 1) * SLAB]


def _s5(ut_ctx, ut_lat, m, e, f, l16, bsz):
    g = m.shape[0]
    rows_ctx, rows_lat = ut_ctx.shape[1], ut_lat.shape[1]
    sq = pl.BlockSpec((1, S5_CW, S5_SW), lambda i: (i, 0, 0))
    return pl.pallas_call(
        functools.partial(_s5_kernel, bsz=bsz, n_ctx=rows_ctx // bsz, n_lat=rows_lat // bsz),
        grid=(g,),
        in_specs=[pl.BlockSpec((HALVES, rows_ctx, SLAB), lambda i: (i, 0, 0)),
                  pl.BlockSpec((HALVES, rows_lat, SLAB), lambda i: (i, 0, 0)), sq, sq, sq,
                  pl.BlockSpec((1, 1, S5_SW), lambda i: (i, 0, 0))],
        out_specs=pl.BlockSpec((HALVES, rows_lat, SLAB), lambda i: (i, 0, 0)),
        out_shape=jax.ShapeDtypeStruct((g * HALVES, rows_lat, SLAB), F32),
        scratch_shapes=[pltpu.VMEM((rows_ctx + rows_lat, S5_SW), F32),
                        pltpu.VMEM((rows_lat, S5_SW), F32),
                        pltpu.VMEM((rows_lat, S5_SW), F32)],
        compiler_params=_params("arbitrary"),
        name="s5",
    )(ut_ctx, ut_lat, m, e, f, l16)


def _rows_to_slab(src_ref, dst_ref):
    n_slab, n_chunks, _ = dst_ref.shape
    for t in range(S5_CHUNK):
        for s in range(n_slab):
            dst_ref[s, :, t * SLAB:(t + 1) * SLAB] = src_ref[s, pl.ds(t, n_chunks, stride=S5_CHUNK), :]


def _slab_to_rows(src_ref, dst_ref):
    n_slab, n_chunks, _ = src_ref.shape
    for t in range(S5_CHUNK):
        for s in range(n_slab):
            dst_ref[s, pl.ds(t, n_chunks, stride=S5_CHUNK), :] = src_ref[s, :, t * SLAB:(t + 1) * SLAB]


def _piece_transpose(tiles):
    piece = lax.broadcasted_iota(jnp.int32, tiles[0].shape, 1) // SSM_GROUP
    tiles = list(tiles)
    k = SLAB_GROUPS // 2
    while k:
        keep_lo = (piece & k) == 0
        for a in range(SLAB_GROUPS):
            if a & k:
                continue
            lo, hi = tiles[a], tiles[a + k]
            tiles[a] = jnp.where(keep_lo, lo, pltpu.roll(hi, k * SSM_GROUP, 1))
            tiles[a + k] = jnp.where(keep_lo, pltpu.roll(lo, SLAB - k * SSM_GROUP, 1), hi)
        k //= 2
    return tiles


def _split_kernel(x_ref, o_ref, *, bsz):
    tc = x_ref.shape[1]
    for b in range(bsz):
        for th in range(HALVES):
            base = b * SLAB_W + th * SLAB_GROUPS * SLAB
            tiles = [x_ref[0, :, base + a * SLAB:base + (a + 1) * SLAB] for a in range(SLAB_GROUPS)]
            for g, tile in enumerate(_piece_transpose(tiles)):
                o_ref[g * HALVES + th, pl.ds(b, tc, stride=bsz), :] = tile


def _split(slabs, tc, bsz):
    n_slab, n_chunks, _ = slabs.shape
    return pl.pallas_call(
        functools.partial(_split_kernel, bsz=bsz),
        grid=(n_slab, n_chunks // tc),
        in_specs=[pl.BlockSpec((1, tc, bsz * SLAB_W), lambda s, r: (s, r, 0))],
        out_specs=pl.BlockSpec((SLAB_GROUPS * HALVES, tc * bsz, SLAB), lambda s, r: (s, r, 0)),
        out_shape=jax.ShapeDtypeStruct((n_slab * SLAB_GROUPS * HALVES, n_chunks * bsz, SLAB), F32),
        compiler_params=_params("arbitrary", "arbitrary"),
        name="split",
    )(slabs)


def _merge_kernel(y_ref, o_ref, *, bsz):
    tc = o_ref.shape[1]
    for b in range(bsz):
        for th in range(HALVES):
            base = b * SLAB_W + th * SLAB_GROUPS * SLAB
            tiles = [y_ref[g * HALVES + th, pl.ds(b, tc, stride=bsz), :] for g in range(SLAB_GROUPS)]
            for a, tile in enumerate(_piece_transpose(tiles)):
                o_ref[0, :, base + a * SLAB:base + (a + 1) * SLAB] = tile


def _merge(y, tc, bsz):
    gh, rows, _ = y.shape
    n_slab = gh // (SLAB_GROUPS * HALVES)
    n_chunks = rows // bsz
    return pl.pallas_call(
        functools.partial(_merge_kernel, bsz=bsz),
        grid=(n_slab, n_chunks // tc),
        in_specs=[pl.BlockSpec((SLAB_GROUPS * HALVES, tc * bsz, SLAB), lambda s, r: (s, r, 0))],
        out_specs=pl.BlockSpec((1, tc, bsz * SLAB_W), lambda s, r: (s, r, 0)),
        out_shape=jax.ShapeDtypeStruct((n_slab, n_chunks, bsz * SLAB_W), F32),
        compiler_params=_params("arbitrary", "arbitrary"),
        name="merge",
    )(y)


def _rope(x, cos, sin_lo, sin_hi):
    return (x * cos + pltpu.roll(x, HEAD_DIM - ROT_HALF // 2, 1) * sin_lo
            + pltpu.roll(x, ROT_HALF // 2, 1) * sin_hi)


def _inproj_kernel(x_ref, sc_ref, sh_ref, w_ref, cos_ref, slo_ref, shi_ref, q_ref, kv_ref, g_ref, us_ref,
                   h_ref, u_ref):
    j = pl.program_id(1)

    @pl.when(j == 0)
    def _():
        h = _layer_norm(x_ref[...]) * (1.0 + sc_ref[0]) + sh_ref[0]
        h_ref[...] = h.astype(BF16)

    acc = jnp.dot(h_ref[...], w_ref[...], preferred_element_type=F32)
    scale = HEAD_DIM ** -0.5

    @pl.when(j == 0)
    def _():
        cos, slo, shi = cos_ref[...], slo_ref[...], shi_ref[...]
        for hd in range(N_HEADS):
            sl = slice(hd * HEAD_DIM, (hd + 1) * HEAD_DIM)
            q_ref[:, sl] = (_rope(acc[:, sl], cos, slo, shi) * scale).astype(BF16)

    @pl.when(j == 1)
    def _():
        cos, slo, shi = cos_ref[...], slo_ref[...], shi_ref[...]
        for hd in range(N_KV_HEADS):
            sl = slice(hd * HEAD_DIM, (hd + 1) * HEAD_DIM)
            kv_ref[:, sl] = _rope(acc[:, sl], cos, slo, shi).astype(BF16)
        kv_ref[:, KV_W:] = acc[:, KV_W:2 * KV_W].astype(BF16)
        for s in range(u_ref.shape[0]):
            u_ref[s] = acc[:, 2 * KV_W + s * SLAB:2 * KV_W + (s + 1) * SLAB]
        _rows_to_slab(u_ref, us_ref)

    @pl.when(j > 1)
    def _():
        g_ref[...] = _sigmoid(acc)


def _inproj(x2, sc, sh, w_in_bf, cos, slo, shi, tm, seq):
    n_tok, d = x2.shape
    tn = Q_W
    n_cols = w_in_bf.shape[1]
    n_slab = (tn - 2 * KV_W) // SLAB
    per_b = seq // tm
    mod_spec = pl.BlockSpec((1, 1, d), lambda i, j: (i // per_b, 0, 0))
    tab_spec = pl.BlockSpec((tm, HEAD_DIM), lambda i, j: (i % per_b, 0))
    return pl.pallas_call(
        _inproj_kernel,
        grid=(n_tok // tm, n_cols // tn),
        in_specs=[pl.BlockSpec((tm, d), lambda i, j: (i, 0)), mod_spec, mod_spec,
                  pl.BlockSpec((d, tn), lambda i, j: (0, j)), tab_spec, tab_spec, tab_spec],
        out_specs=[pl.BlockSpec((tm, Q_W), lambda i, j: (i, 0)),
                   pl.BlockSpec((tm, 2 * KV_W), lambda i, j: (i, 0)),
                   pl.BlockSpec((tm, tn), lambda i, j: (i, jnp.maximum(j - 2, 0))),
                   pl.BlockSpec((n_slab, tm // S5_CHUNK, SLAB_W), lambda i, j: (0, i % per_b, i // per_b))],
        out_shape=[jax.ShapeDtypeStruct((n_tok, Q_W), BF16),
                   jax.ShapeDtypeStruct((n_tok, 2 * KV_W), BF16),
                   jax.ShapeDtypeStruct((n_tok, n_cols - 2 * tn), F32),
                   jax.ShapeDtypeStruct((n_slab, seq // S5_CHUNK, (n_tok // seq) * SLAB_W), F32)],
        scratch_shapes=[pltpu.VMEM((tm, d), BF16), pltpu.VMEM((n_slab, tm, SLAB), F32)],
        compiler_params=_params("arbitrary", "arbitrary"),
        name="inproj",
    )(x2, sc, sh, w_in_bf, cos, slo, shi)


def _ctxproj_kernel(x_ref, sc_ref, sh_ref, w_ref, kv_ref, us_ref, u_ref):
    h = _layer_norm(x_ref[...]) * (1.0 + sc_ref[...]) + sh_ref[...]
    acc = jnp.dot(h.astype(BF16), w_ref[...], preferred_element_type=F32)
    kv_ref[...] = acc[:, :2 * KV_W].astype(BF16)
    for s in range(u_ref.shape[0]):
        u_ref[s] = acc[:, 2 * KV_W + s * SLAB:2 * KV_W + (s + 1) * SLAB]
    _rows_to_slab(u_ref, us_ref)


def _ctxproj(ctx2, sc, sh, w_in_bf, tm):
    n_tok, d = ctx2.shape
    tn = Q_W
    n_slab = (tn - 2 * KV_W) // SLAB
    vec = pl.BlockSpec((1, d), lambda i: (0, 0))
    return pl.pallas_call(
        _ctxproj_kernel,
        grid=(n_tok // tm,),
        in_specs=[pl.BlockSpec((tm, d), lambda i: (i, 0)), vec, vec,
                  pl.BlockSpec((d, tn), lambda i: (0, 1))],
        out_specs=[pl.BlockSpec((tm, 2 * KV_W), lambda i: (i, 0)),
                   pl.BlockSpec((n_slab, tm // S5_CHUNK, SLAB_W), lambda i: (0, 0, i))],
        out_shape=[jax.ShapeDtypeStruct((n_tok, 2 * KV_W), BF16),
                   jax.ShapeDtypeStruct((n_slab, tm // S5_CHUNK, (n_tok // tm) * SLAB_W), F32)],
        scratch_shapes=[pltpu.VMEM((n_slab, tm, SLAB), F32)],
        compiler_params=_params("arbitrary"),
        name="ctxproj",
    )(ctx2, sc, sh, w_in_bf)


def _attn_kernel(sink_ref, q_ref, kv_ref, ckv_ref, bias_ref, o_ref, *, seq):
    n = pl.program_id(1)
    start = pl.multiple_of(jnp.clip((n - 1) * BLOCK, 0, seq - BAND), BLOCK)
    keys = jnp.concatenate([kv_ref[pl.ds(start, BAND), :], ckv_ref[...]], 0)
    bias = bias_ref[0]
    ones = jnp.ones((keys.shape[0], HEAD_DIM), BF16)
    for kv in range(N_KV_HEADS):
        heads = [kv * GROUP + g for g in range(GROUP)]
        k = keys[:, kv * HEAD_DIM:(kv + 1) * HEAD_DIM]
        v1 = jnp.concatenate([keys[:, KV_W + kv * HEAD_DIM:KV_W + (kv + 1) * HEAD_DIM], ones], 1)
        q = jnp.concatenate([q_ref[:, h * HEAD_DIM:(h + 1) * HEAD_DIM] for h in heads], 0)
        s = lax.dot_general(q, k, NT_DIMS, preferred_element_type=F32) + bias
        sink = jnp.concatenate([jnp.full((BLOCK, 1), sink_ref[h], F32) for h in heads], 0)
        m = jnp.maximum(jnp.max(s, -1, keepdims=True), sink)
        p = jnp.exp(s - m).astype(BF16)
        pv = jnp.dot(p, v1, preferred_element_type=F32)
        o = pv[:, :HEAD_DIM] / (pv[:, HEAD_DIM:] + jnp.exp(sink - m))
        for g, h in enumerate(heads):
            o_ref[:, h * HEAD_DIM:(h + 1) * HEAD_DIM] = o[g * BLOCK:(g + 1) * BLOCK].astype(o_ref.dtype)


def _attn_bias(n_ctx_tok):
    ql = jnp.arange(BLOCK)[:, None]
    col = jnp.arange(BAND)[None, :]
    rel = jnp.array([0, -BLOCK, -2 * BLOCK])[:, None, None]
    ok = jnp.abs(ql - (rel + col)) <= WINDOW
    ok = jnp.concatenate([ok, jnp.ones((3, BLOCK, n_ctx_tok), bool)], -1)
    return jnp.tile(jnp.where(ok, 0.0, NEG_INF).astype(F32), (1, GROUP, 1))


def _attn(sink, q, kv, ckv, bsz, seq, n_ctx_tok):
    nb = seq // BLOCK
    n_keys = BAND + n_ctx_tok

    def placement(b, n, s):
        return (jnp.where(n == 0, 0, jnp.where(n == nb - 1, 2, 1)), 0, 0)

    return pl.pallas_call(
        functools.partial(_attn_kernel, seq=seq),
        grid_spec=pltpu.PrefetchScalarGridSpec(
            num_scalar_prefetch=1,
            grid=(bsz, nb),
            in_specs=[pl.BlockSpec((BLOCK, Q_W), lambda b, n, s: (b * nb + n, 0)),
                      pl.BlockSpec((seq, 2 * KV_W), lambda b, n, s: (b, 0)),
                      pl.BlockSpec((n_ctx_tok, 2 * KV_W), lambda b, n, s: (b, 0)),
                      pl.BlockSpec((1, GROUP * BLOCK, n_keys), placement)],
            out_specs=pl.BlockSpec((BLOCK, Q_W), lambda b, n, s: (b * nb + n, 0)),
        ),
        out_shape=jax.ShapeDtypeStruct((bsz * seq, Q_W), BF16),
        compiler_params=_params("arbitrary", "arbitrary"),
        name="attn",
    )(sink, q, kv, ckv, _attn_bias(n_ctx_tok))


def _gelu_tanh(x):
    return 0.5 * x * (1.0 + jnp.tanh(0.7978845608028654 * (x + 0.044715 * (x * x * x))))


def _mix_kernel(a_ref, ys_ref, ga_ref, gs_ref, wg_ref, wa_ref, ws_ref, o_ref, y_ref, *, ssm_w):
    _slab_to_rows(ys_ref, y_ref)
    y = jnp.concatenate([y_ref[s] for s in range(y_ref.shape[0])], 1)
    z = jnp.dot(_gelu_tanh(y).astype(BF16), wg_ref[...], preferred_element_type=F32)
    glu = (z[:, :ssm_w] * _sigmoid(z[:, ssm_w:])).astype(BF16)
    attn_d = jnp.dot(a_ref[...], wa_ref[...], preferred_element_type=F32)
    ssm_d = jnp.dot(glu, ws_ref[...], preferred_element_type=F32)
    o_ref[...] = (ga_ref[...] * attn_d + gs_ref[...] * ssm_d).astype(o_ref.dtype)


def _mix(attn, y_slabs, gates, w_glu_bf, w_au_bf, w_su_bf, tm, d, seq):
    n_tok = attn.shape[0]
    n_slab = y_slabs.shape[0]
    ssm_w = n_slab * SLAB
    per_b = seq // tm
    full = lambda a: pl.BlockSpec(a.shape, lambda i: (0, 0))
    return pl.pallas_call(
        functools.partial(_mix_kernel, ssm_w=ssm_w),
        grid=(n_tok // tm,),
        in_specs=[pl.BlockSpec((tm, Q_W), lambda i: (i, 0)),
                  pl.BlockSpec((n_slab, tm // S5_CHUNK, SLAB_W), lambda i: (0, i % per_b, i // per_b)),
                  pl.BlockSpec((tm, d), lambda i: (i, 0)),
                  pl.BlockSpec((tm, d), lambda i: (i, 1)),
                  full(w_glu_bf), full(w_au_bf), full(w_su_bf)],
        out_specs=pl.BlockSpec((tm, d), lambda i: (i, 0)),
        out_shape=jax.ShapeDtypeStruct((n_tok, d), BF16),
        scratch_shapes=[pltpu.VMEM((n_slab, tm, SLAB), F32)],
        compiler_params=_params("arbitrary"),
        name="mix",
    )(attn, y_slabs, gates, gates, w_glu_bf, w_au_bf, w_su_bf)


def _outproj_kernel(m_ref, x_ref, w_ref, g1_ref, lg_ref, lb_ref, sc_ref, sh_ref, x1_ref, h2_ref, *, alpha):
    y = jnp.dot(m_ref[...], w_ref[...], preferred_element_type=F32)
    x1 = _layer_norm(alpha * x_ref[...] + g1_ref[0] * y) * lg_ref[...] + lb_ref[...]
    x1_ref[...] = x1
    h2_ref[...] = (_layer_norm(x1) * (1.0 + sc_ref[0]) + sh_ref[0]).astype(h2_ref.dtype)


def _outproj(mixed, x2, w_out_bf, g1, ln_g, ln_b, sc2, sh2, tm, seq, alpha):
    n_tok, d = x2.shape
    per_b = seq // tm
    rows = pl.BlockSpec((tm, d), lambda i: (i, 0))
    mod_spec = pl.BlockSpec((1, 1, d), lambda i: (i // per_b, 0, 0))
    vec = pl.BlockSpec((1, d), lambda i: (0, 0))
    return pl.pallas_call(
        functools.partial(_outproj_kernel, alpha=alpha),
        grid=(n_tok // tm,),
        in_specs=[rows, rows, pl.BlockSpec((d, d), lambda i: (0, 0)), mod_spec, vec, vec, mod_spec, mod_spec],
        out_specs=[rows, rows],
        out_shape=[jax.ShapeDtypeStruct((n_tok, d), F32), jax.ShapeDtypeStruct((n_tok, d), BF16)],
        compiler_params=_params("arbitrary"),
        name="outproj",
    )(mixed, x2, w_out_bf, g1, ln_g, ln_b, sc2, sh2)


def _mlp_kernel(h_ref, x_ref, w1_ref, b1_ref, w2_ref, b2_ref, g2_ref, lg_ref, lb_ref, o_ref, acc_ref, *, alpha):
    f = pl.program_id(1)
    a = jnp.dot(h_ref[...], w1_ref[...], preferred_element_type=F32) + b1_ref[...]
    a = jnp.maximum(a, 0.0)
    part = jnp.dot((a * a).astype(BF16), w2_ref[...], preferred_element_type=F32)

    @pl.when(f == 0)
    def _():
        acc_ref[...] = part

    @pl.when(f > 0)
    def _():
        acc_ref[...] += part

    @pl.when(f == pl.num_programs(1) - 1)
    def _():
        y = acc_ref[...] + b2_ref[...]
        o_ref[...] = _layer_norm(alpha * x_ref[...] + g2_ref[0] * y) * lg_ref[...] + lb_ref[...]


def _mlp(h2, x1, w1_bf, b1, w2_bf, b2, g2, ln_g, ln_b, tm, tf, seq, alpha):
    n_tok, d = x1.shape
    d_ff = w1_bf.shape[1]
    per_b = seq // tm
    rows = pl.BlockSpec((tm, d), lambda i, f: (i, 0))
    vec = pl.BlockSpec((1, d), lambda i, f: (0, 0))
    return pl.pallas_call(
        functools.partial(_mlp_kernel, alpha=alpha),
        grid=(n_tok // tm, d_ff // tf),
        in_specs=[rows, rows,
                  pl.BlockSpec((d, tf), lambda i, f: (0, f)),
                  pl.BlockSpec((1, tf), lambda i, f: (0, f)),
                  pl.BlockSpec((tf, d), lambda i, f: (f, 0)),
                  vec, pl.BlockSpec((1, 1, d), lambda i, f: (i // per_b, 0, 0)), vec, vec],
        out_specs=rows,
        out_shape=jax.ShapeDtypeStruct((n_tok, d), F32),
        scratch_shapes=[pltpu.VMEM((tm, d), F32)],
        compiler_params=_params("arbitrary", "arbitrary"),
        name="mlp",
    )(h2, x1, w1_bf, b1, w2_bf, b2, g2, ln_g, ln_b)


def _rope_tables(seq):
    rows = seq // GRID_W
    row = jnp.repeat(jnp.arange(rows), GRID_W)
    col = jnp.tile(jnp.arange(GRID_W), rows)
    n_freq = ROT_HALF // 2
    freqs = ROPE_BASE ** (-jnp.arange(n_freq, dtype=F32) / n_freq)
    ang_r = row.astype(F32)[:, None] * freqs
    ang_c = col.astype(F32)[:, None] * freqs
    ang = jnp.concatenate([ang_r, ang_r, ang_c, ang_c], -1)
    cos, sin = jnp.cos(ang), jnp.sin(ang)
    low = (jnp.arange(HEAD_DIM) % ROT_HALF) < n_freq
    return cos, jnp.where(low, -sin, 0.0), jnp.where(low, 0.0, sin)


def _tiles(seq):
    return dict(mod_tn=1024, proj_tm=min(512, seq), row_tm=min(512, seq), mlp_tf=1024, slab_tc=16)


def kernel(x, c, ctx, c_ctx, w_ada, b_ada, w_in, attn_sink, ssm_a_re, ssm_a_im, ssm_log_dt, ssm_b_re, ssm_b_im,
           ssm_c_re, ssm_c_im, ssm_d, w_glu, w_attn_up, w_ssm_up, w_out, ln_mix_g, ln_mix_b, w_mlp1, b_mlp1,
           w_mlp2, b_mlp2, ln_mlp_g, ln_mlp_b):
    depth = w_ada.shape[0]
    assert depth == 1, "single-layer problem: the context stream is never updated"
    bsz, seq, d = x.shape
    n_ctx_tok = ctx.shape[1]
    groups = ssm_a_re.shape[2]
    ssm_w = groups * SSM_GROUP
    assert w_in.shape[2] == Q_W + 2 * KV_W + ssm_w + 2 * d and Q_W + 2 * KV_W + ssm_w == d
    assert Q_W == 2 * KV_W + ssm_w and d % Q_W == 0, "w_in column tiles: q | k,v,u | gates"
    assert seq % BLOCK == 0 and seq >= BAND and n_ctx_tok % S5_CHUNK == 0 and bsz == 8
    alpha = (2.0 * depth) ** 0.25
    t = _tiles(seq)

    pad = jnp.zeros((16 - bsz - 1, d), F32)
    mod = _mod(jnp.concatenate([c, c_ctx[None], pad], 0), w_ada[0], b_ada[0][None], t["mod_tn"])
    sh1, sc1, g1, sh2, sc2, g2 = [mod[:bsz, i * d:(i + 1) * d][:, None, :] for i in range(6)]
    csh1, csc1 = mod[bsz:bsz + 1, :d], mod[bsz:bsz + 1, d:2 * d]

    w_in_bf = w_in[0].astype(BF16)
    x2 = x.reshape(bsz * seq, d)
    cos, slo, shi = _rope_tables(seq)
    q, kv, gates, u_lat = _inproj(x2, sc1, sh1, w_in_bf, cos, slo, shi, t["proj_tm"], seq)
    ckv, u_ctx = _ctxproj(ctx.reshape(bsz * n_ctx_tok, d), csc1, csh1, w_in_bf, n_ctx_tok)

    attn = _attn(attn_sink[0], q, kv, ckv, bsz, seq, n_ctx_tok)

    m, e, f, l16 = _s5prep(ssm_a_re[0], ssm_a_im[0], ssm_log_dt[0], ssm_b_re[0], ssm_b_im[0],
                           ssm_c_re[0], ssm_c_im[0], ssm_d[0])
    ut_ctx = _split(u_ctx, t["slab_tc"], bsz)
    ut_lat = _split(u_lat, t["slab_tc"], bsz)
    y_slabs = _merge(_s5(ut_ctx, ut_lat, m, e, f, l16, bsz), t["slab_tc"], bsz)

    mixed = _mix(attn, y_slabs, gates, w_glu[0].astype(BF16), w_attn_up[0].astype(BF16), w_ssm_up[0].astype(BF16),
                 t["row_tm"], d, seq)
    x1, h2 = _outproj(mixed, x2, w_out[0].astype(BF16), g1, ln_mix_g, ln_mix_b, sc2, sh2, t["row_tm"], seq, alpha)
    out = _mlp(h2, x1, w_mlp1[0].astype(BF16), b_mlp1, w_mlp2[0].astype(BF16), b_mlp2, g2, ln_mlp_g, ln_mlp_b,
               t["row_tm"], t["mlp_tf"], seq, alpha)
    return out.reshape(bsz, seq, d)
```

```python
import functools

import jax
import jax.numpy as jnp
from jax import lax
from jax.experimental import pallas as pl
from jax.experimental.pallas import tpu as pltpu

F32 = jnp.float32
BF16 = jnp.bfloat16

GRID_W = 64
HEAD_DIM = 128
N_HEADS = 8
N_KV_HEADS = 2
GROUP = N_HEADS // N_KV_HEADS
WINDOW = 128
BLOCK = 128
BAND = 3 * BLOCK
ROT_HALF = HEAD_DIM // 2
ROPE_BASE = 10000.0
SSM_GROUP = 16
SSM_STATE = 64
LN_EPS = 1e-6
NEG_INF = -1e30
Q_W = N_HEADS * HEAD_DIM
KV_W = N_KV_HEADS * HEAD_DIM

S5_CHUNK = 16
S5_CW = S5_CHUNK * SSM_GROUP
S5_SW = 4 * SSM_STATE
SLAB = 128
SLAB_GROUPS = SLAB // SSM_GROUP
SLAB_W = S5_CHUNK * SLAB
HALVES = S5_CW // SLAB
PROJ_SUB = 256

V7X_VMEM_BYTES = 64 * 1024 * 1024
V7X_VMEM_LIMIT_BYTES = V7X_VMEM_BYTES - 4 * 1024 * 1024
HI = lax.Precision.HIGHEST
NT_DIMS = (((1,), (1,)), ((), ()))


def _params(*semantics):
    return pltpu.CompilerParams(dimension_semantics=semantics, vmem_limit_bytes=V7X_VMEM_LIMIT_BYTES)


def _layer_norm(x):
    mu = jnp.mean(x, -1, keepdims=True)
    xc = x - mu
    var = jnp.mean(xc * xc, -1, keepdims=True)
    return xc * lax.rsqrt(var + LN_EPS)


def _sigmoid(x):
    return 1.0 / (1.0 + jnp.exp(-x))


def _mod_kernel(c_ref, w_ref, b_ref, o_ref):
    c = c_ref[...]
    s = (c * _sigmoid(c)).astype(BF16)
    o_ref[...] = jnp.dot(s, w_ref[...].astype(BF16), preferred_element_type=F32) + b_ref[...]


def _mod(cc, w_ada, b_ada, tn):
    rows, d = cc.shape
    n = w_ada.shape[1]
    return pl.pallas_call(
        _mod_kernel,
        grid=(n // tn,),
        in_specs=[pl.BlockSpec((rows, d), lambda j: (0, 0)),
                  pl.BlockSpec((d, tn), lambda j: (0, j)),
                  pl.BlockSpec((1, tn), lambda j: (0, j))],
        out_specs=pl.BlockSpec((rows, tn), lambda j: (0, j)),
        out_shape=jax.ShapeDtypeStruct((rows, n), F32),
        compiler_params=_params("arbitrary"),
        name="mod",
    )(cc, w_ada, b_ada)


def _s5prep_kernel(are_ref, aim_ref, ldt_ref, br_ref, bi_ref, cr_ref, ci_ref, d_ref,
                   m_ref, e_ref, f_ref, l_ref):
    ar = are_ref[0]
    ai = aim_ref[0]
    dt = jnp.exp(ldt_ref[0])
    dar = dt * ar
    dai = dt * ai
    lane = lax.broadcasted_iota(jnp.int32, (S5_CHUNK, 128), 1)
    jj = lax.broadcasted_iota(jnp.int32, (S5_CHUNK, 128), 0)
    fwd = lane < SSM_STATE

    def powers(k):
        kf = k.astype(F32)
        mag = jnp.exp(kf * dar)
        ang = kf * dai
        return mag * jnp.cos(ang), mag * jnp.sin(ang)

    l1r, l1i = powers(jnp.ones((1, 128), jnp.int32))
    x = l1r - 1.0
    den = ar * ar + ai * ai
    cfr = (x * ar + l1i * ai) / den
    cfi = (l1i * ar - x * ai) / den
    br = br_ref[0]
    bi = bi_ref[0]
    bbr = cfr * br - cfi * bi
    bbi = cfr * bi + cfi * br
    cr = cr_ref[0]
    ci = ci_ref[0]

    def expand(tab):
        return jnp.concatenate(
            [jnp.broadcast_to(tab[j:j + 1], (SSM_GROUP, 128)) for j in range(S5_CHUNK)], 0)

    def tile(xm):
        return jnp.concatenate([xm] * S5_CHUNK, 0)

    def cprod(k, xr, xi):
        tr, ti = powers(k)
        tr, ti = expand(tr), expand(ti)
        xr, xi = tile(xr), tile(xi)
        return tr * xr - ti * xi, tr * xi + ti * xr

    last = S5_CHUNK - 1
    er, ei = cprod(jnp.where(fwd, last - jj, jj), bbr, bbi)
    e_ref[0] = jnp.concatenate([er, ei], 1)
    fr, fi = cprod(jnp.where(fwd, jj + 1, S5_CHUNK - jj), cr, ci)
    f_ref[0] = jnp.concatenate([fr, -fi], 1)
    lr, li = powers(jnp.full((1, 128), S5_CHUNK, jnp.int32))
    l_ref[0] = jnp.concatenate([lr, li], 1)

    rr, ri = cprod(jnp.where(fwd, jj, last - jj), cr, ci)
    rt = jnp.concatenate([rr, -ri], 1)
    bb = jnp.concatenate([bbr, bbi], 1)
    fwd2 = jnp.concatenate([fwd, fwd], 1)
    kf = lax.dot_general(jnp.where(fwd2, bb, 0.0), rt, NT_DIMS, precision=HI, preferred_element_type=F32)
    kb = lax.dot_general(jnp.where(fwd2, 0.0, bb), rt, NT_DIMS, precision=HI, preferred_element_type=F32)

    lane2 = lax.broadcasted_iota(jnp.int32, (SSM_GROUP, S5_CW), 1)
    hh = lax.broadcasted_iota(jnp.int32, (SSM_GROUP, S5_CW), 0)
    dsk = d_ref[0]
    for s in range(S5_CHUNK):
        lo = SSM_GROUP * s
        hi = SSM_GROUP * (s + 1)
        kfs = kf if s == 0 else pltpu.roll(kf, lo, 1)
        kbs = kb if s == last else pltpu.roll(kb, hi, 1)
        blk = jnp.where(lane2 >= lo, kfs, 0.0) + jnp.where(lane2 < hi, kbs, 0.0)
        blk = blk + jnp.where(lane2 == lo + hh, dsk, 0.0)
        m_ref[0, lo:hi, :] = blk


def _s5prep(a_re, a_im, log_dt, b_re, b_im, c_re, c_im, d_skip):
    g = a_re.shape[1]

    def fb(t):
        return jnp.concatenate([t[0], t[1]], -1)

    are = fb(a_re[:, :, None, :])
    aim = fb(a_im[:, :, None, :])
    ldt = fb(jnp.broadcast_to(log_dt[:, :, None, None], (2, g, 1, SSM_STATE)))
    br = fb(jnp.swapaxes(b_re, -1, -2))
    bi = fb(jnp.swapaxes(b_im, -1, -2))
    cr = fb(c_re)
    ci = fb(c_im)
    dsk = jnp.tile(d_skip, (1, S5_CHUNK))[:, None, :]

    row = pl.BlockSpec((1, 1, 128), lambda i: (i, 0, 0))
    mat = pl.BlockSpec((1, SSM_GROUP, 128), lambda i: (i, 0, 0))
    sq = pl.BlockSpec((1, S5_CW, S5_SW), lambda i: (i, 0, 0))
    return pl.pallas_call(
        _s5prep_kernel,
        grid=(g,),
        in_specs=[row, row, row, mat, mat, mat, mat, pl.BlockSpec((1, 1, S5_CW), lambda i: (i, 0, 0))],
        out_specs=[sq, sq, sq, pl.BlockSpec((1, 1, S5_SW), lambda i: (i, 0, 0))],
        out_shape=[jax.ShapeDtypeStruct((g, S5_CW, S5_CW), F32),
                   jax.ShapeDtypeStruct((g, S5_CW, S5_SW), F32),
                   jax.ShapeDtypeStruct((g, S5_CW, S5_SW), F32),
                   jax.ShapeDtypeStruct((g, 1, S5_SW), F32)],
        compiler_params=_params("arbitrary"),
        name="s5prep",
    )(are, aim, ldt, br, bi, cr, ci, dsk)


def _s5_kernel(uc_ref, ul_ref, m_ref, e_ref, f_ref, l_ref, y_ref, s_ref, sf_ref, sb_ref, *, bsz, n_ctx, n_lat):
    u_lat = jnp.concatenate([ul_ref[h] for h in range(HALVES)], 1).astype(BF16)
    u_ctx = jnp.concatenate([uc_ref[h] for h in range(HALVES)], 1).astype(BF16)
    e = e_ref[0].astype(BF16)
    s_ref[:n_ctx * bsz, :] = jnp.dot(u_ctx, e, preferred_element_type=F32)
    s_ref[n_ctx * bsz:, :] = jnp.dot(u_lat, e, preferred_element_type=F32)
    half = S5_SW // 2
    lr = jnp.broadcast_to(l_ref[0][:, :half], (bsz, half))
    li = jnp.broadcast_to(l_ref[0][:, half:], (bsz, half))
    fwd = lax.broadcasted_iota(jnp.int32, (bsz, half), 1) < SSM_STATE
    n_all = n_ctx + n_lat

    def rows(c):
        return pl.ds(pl.multiple_of(c * bsz, bsz), bsz)

    def advance(cf, cb, carry):
        c_re, c_im = carry
        sf = s_ref[rows(cf), :]
        sb = s_ref[rows(cb), :]
        x_re = jnp.where(fwd, sf[:, :half], sb[:, :half])
        x_im = jnp.where(fwd, sf[:, half:], sb[:, half:])
        return lr * c_re - li * c_im + x_re, lr * c_im + li * c_re + x_im

    def ctx_step(i, carry):
        return advance(i, n_ctx - 1 - i, carry)

    def lat_step(i, carry):
        c_re, c_im = carry
        both = jnp.concatenate([c_re, c_im], 1)
        sf_ref[rows(i), :] = both
        sb_ref[rows(n_lat - 1 - i), :] = both
        return advance(n_ctx + i, n_all - 1 - i, carry)

    zero = jnp.zeros((bsz, half), F32)
    carry = lax.fori_loop(0, n_ctx, ctx_step, (zero, zero))
    lax.fori_loop(0, n_lat, lat_step, carry)

    fwd2 = (lax.broadcasted_iota(jnp.int32, (1, S5_SW), 1) % half) < SSM_STATE
    s_in = jnp.where(fwd2, sf_ref[...], sb_ref[...]).astype(BF16)
    y = (jnp.dot(u_lat, m_ref[0].astype(BF16), preferred_element_type=F32)
         + lax.dot_general(s_in, f_ref[0].astype(BF16), NT_DIMS, preferred_element_type=F32))
    for h in range(HALVES):
        y_ref[h] = y[:, h * SLAB:(h + 1) * SLAB]


def _s5(ut_ctx, ut_lat, m, e, f, l16, bsz):
    g = m.shape[0]
    rows_ctx, rows_lat = ut_ctx.shape[1], ut_lat.shape[1]
    sq = pl.BlockSpec((1, S5_CW, S5_SW), lambda i: (i, 0, 0))
    return pl.pallas_call(
        functools.partial(_s5_kernel, bsz=bsz, n_ctx=rows_ctx // bsz, n_lat=rows_lat // bsz),
        grid=(g,),
        in_specs=[pl.BlockSpec((HALVES, rows_ctx, SLAB), lambda i: (i, 0, 0)),
                  pl.BlockSpec((HALVES, rows_lat, SLAB), lambda i: (i, 0, 0)), sq, sq, sq,
                  pl.BlockSpec((1, 1, S5_SW), lambda i: (i, 0, 0))],
        out_specs=pl.BlockSpec((HALVES, rows_lat, SLAB), lambda i: (i, 0, 0)),
        out_shape=jax.ShapeDtypeStruct((g * HALVES, rows_lat, SLAB), F32),
        scratch_shapes=[pltpu.VMEM((rows_ctx + rows_lat, S5_SW), F32),
                        pltpu.VMEM((rows_lat, S5_SW), F32),
                        pltpu.VMEM((rows_lat, S5_SW), F32)],
        compiler_params=_params("arbitrary"),
        name="s5",
    )(ut_ctx, ut_lat, m, e, f, l16)


def _rows_to_slab(src_ref, dst_ref):
    n_slab, n_chunks, _ = dst_ref.shape
    for t in range(S5_CHUNK):
        for s in range(n_slab):
            dst_ref[s, :, t * SLAB:(t + 1) * SLAB] = src_ref[s, pl.ds(t, n_chunks, stride=S5_CHUNK), :]


def _slab_to_rows(src_ref, dst_ref):
    n_slab, n_chunks, _ = src_ref.shape
    for t in range(S5_CHUNK):
        for s in range(n_slab):
            dst_ref[s, pl.ds(t, n_chunks, stride=S5_CHUNK), :] = src_ref[s, :, t * SLAB:(t + 1) * SLAB]


def _piece_transpose(tiles):
    piece = lax.broadcasted_iota(jnp.int32, tiles[0].shape, 1) // SSM_GROUP
    tiles = list(tiles)
    k = SLAB_GROUPS // 2
    while k:
        keep_lo = (piece & k) == 0
        for a in range(SLAB_GROUPS):
            if a & k:
                continue
            lo, hi = tiles[a], tiles[a + k]
            tiles[a] = jnp.where(keep_lo, lo, pltpu.roll(hi, k * SSM_GROUP, 1))
            tiles[a + k] = jnp.where(keep_lo, pltpu.roll(lo, SLAB - k * SSM_GROUP, 1), hi)
        k //= 2
    return tiles


def _split_kernel(x_ref, o_ref, *, bsz):
    tc = x_ref.shape[1]
    for b in range(bsz):
        for th in range(HALVES):
            base = b * SLAB_W + th * SLAB_GROUPS * SLAB
            tiles = [x_ref[0, :, base + a * SLAB:base + (a + 1) * SLAB] for a in range(SLAB_GROUPS)]
            for g, tile in enumerate(_piece_transpose(tiles)):
                o_ref[g * HALVES + th, pl.ds(b, tc, stride=bsz), :] = tile


def _split(slabs, tc, bsz):
    n_slab, n_chunks, _ = slabs.shape
    tc = min(tc, n_chunks)
    return pl.pallas_call(
        functools.partial(_split_kernel, bsz=bsz),
        grid=(n_slab, n_chunks // tc),
        in_specs=[pl.BlockSpec((1, tc, bsz * SLAB_W), lambda s, r: (s, r, 0))],
        out_specs=pl.BlockSpec((SLAB_GROUPS * HALVES, tc * bsz, SLAB), lambda s, r: (s, r, 0)),
        out_shape=jax.ShapeDtypeStruct((n_slab * SLAB_GROUPS * HALVES, n_chunks * bsz, SLAB), F32),
        compiler_params=_params("arbitrary", "arbitrary"),
        name="split",
    )(slabs)


def _merge_kernel(y_ref, o_ref, *, bsz):
    tc = o_ref.shape[1]
    for b in range(bsz):
        for th in range(HALVES):
            base = b * SLAB_W + th * SLAB_GROUPS * SLAB
            tiles = [y_ref[g * HALVES + th, pl.ds(b, tc, stride=bsz), :] for g in range(SLAB_GROUPS)]
            for a, tile in enumerate(_piece_transpose(tiles)):
                o_ref[0, :, base + a * SLAB:base + (a + 1) * SLAB] = tile


def _merge(y, tc, bsz):
    gh, rows, _ = y.shape
    n_slab = gh // (SLAB_GROUPS * HALVES)
    n_chunks = rows // bsz
    tc = min(tc, n_chunks)
    return pl.pallas_call(
        functools.partial(_merge_kernel, bsz=bsz),
        grid=(n_slab, n_chunks // tc),
        in_specs=[pl.BlockSpec((SLAB_GROUPS * HALVES, tc * bsz, SLAB), lambda s, r: (s, r, 0))],
        out_specs=pl.BlockSpec((1, tc, bsz * SLAB_W), lambda s, r: (s, r, 0)),
        out_shape=jax.ShapeDtypeStruct((n_slab, n_chunks, bsz * SLAB_W), F32),
        compiler_params=_params("arbitrary", "arbitrary"),
        name="merge",
    )(y)


def _rope(x, cos, sin_lo, sin_hi):
    return (x * cos + pltpu.roll(x, HEAD_DIM - ROT_HALF // 2, 1) * sin_lo
            + pltpu.roll(x, ROT_HALF // 2, 1) * sin_hi)


def _inproj_kernel(x_ref, sc_ref, sh_ref, w_ref, cos_ref, slo_ref, shi_ref, q_ref, kv_ref, g_ref, us_ref,
                   h_ref, u_ref):
    j = pl.program_id(1)

    @pl.when(j == 0)
    def _():
        h = _layer_norm(x_ref[...]) * (1.0 + sc_ref[0]) + sh_ref[0]
        h_ref[...] = h.astype(BF16)

    scale = HEAD_DIM ** -0.5
    n_sub = w_ref.shape[1] // PROJ_SUB

    def sub_dot(c):
        return jnp.dot(h_ref[...], w_ref[:, c * PROJ_SUB:(c + 1) * PROJ_SUB], preferred_element_type=F32)

    def lane_tiles(acc):
        return [acc[:, k * HEAD_DIM:(k + 1) * HEAD_DIM] for k in range(PROJ_SUB // HEAD_DIM)]

    @pl.when(j == 0)
    def _():
        cos, slo, shi = cos_ref[...], slo_ref[...], shi_ref[...]
        for c in range(n_sub):
            for k, xh in enumerate(lane_tiles(sub_dot(c))):
                col = c * PROJ_SUB + k * HEAD_DIM
                q_ref[:, col:col + HEAD_DIM] = (_rope(xh, cos, slo, shi) * scale).astype(BF16)

    @pl.when(j == 1)
    def _():
        cos, slo, shi = cos_ref[...], slo_ref[...], shi_ref[...]
        for c in range(n_sub):
            for k, xh in enumerate(lane_tiles(sub_dot(c))):
                col = c * PROJ_SUB + k * HEAD_DIM
                if col < KV_W:
                    kv_ref[:, col:col + HEAD_DIM] = _rope(xh, cos, slo, shi).astype(BF16)
                elif col < 2 * KV_W:
                    kv_ref[:, col:col + HEAD_DIM] = xh.astype(BF16)
                else:
                    u_ref[(col - 2 * KV_W) // SLAB] = xh
        _rows_to_slab(u_ref, us_ref)

    @pl.when(j > 1)
    def _():
        for c in range(n_sub):
            g_ref[:, c * PROJ_SUB:(c + 1) * PROJ_SUB] = _sigmoid(sub_dot(c))


def _inproj(x2, sc, sh, w_in_bf, cos, slo, shi, tm, seq):
    n_tok, d = x2.shape
    tn = Q_W
    n_cols = w_in_bf.shape[1]
    n_slab = (tn - 2 * KV_W) // SLAB
    per_b = seq // tm
    mod_spec = pl.BlockSpec((1, 1, d), lambda i, j: (i // per_b, 0, 0))
    tab_spec = pl.BlockSpec((tm, HEAD_DIM), lambda i, j: (i % per_b, 0))
    return pl.pallas_call(
        _inproj_kernel,
        grid=(n_tok // tm, n_cols // tn),
        in_specs=[pl.BlockSpec((tm, d), lambda i, j: (i, 0)), mod_spec, mod_spec,
                  pl.BlockSpec((d, tn), lambda i, j: (0, j)), tab_spec, tab_spec, tab_spec],
        out_specs=[pl.BlockSpec((tm, Q_W), lambda i, j: (i, 0)),
                   pl.BlockSpec((tm, 2 * KV_W), lambda i, j: (i, 0)),
                   pl.BlockSpec((tm, tn), lambda i, j: (i, jnp.maximum(j - 2, 0))),
                   pl.BlockSpec((n_slab, tm // S5_CHUNK, SLAB_W), lambda i, j: (0, i % per_b, i // per_b))],
        out_shape=[jax.ShapeDtypeStruct((n_tok, Q_W), BF16),
                   jax.ShapeDtypeStruct((n_tok, 2 * KV_W), BF16),
                   jax.ShapeDtypeStruct((n_tok, n_cols - 2 * tn), F32),
                   jax.ShapeDtypeStruct((n_slab, seq // S5_CHUNK, (n_tok // seq) * SLAB_W), F32)],
        scratch_shapes=[pltpu.VMEM((tm, d), BF16), pltpu.VMEM((n_slab, tm, SLAB), F32)],
        compiler_params=_params("arbitrary", "arbitrary"),
        name="inproj",
    )(x2, sc, sh, w_in_bf, cos, slo, shi)


def _ctxproj_kernel(x_ref, sc_ref, sh_ref, w_ref, kv_ref, us_ref, u_ref):
    h = _layer_norm(x_ref[...]) * (1.0 + sc_ref[...]) + sh_ref[...]
    acc = jnp.dot(h.astype(BF16), w_ref[...], preferred_element_type=F32)
    kv_ref[...] = acc[:, :2 * KV_W].astype(BF16)
    for s in range(u_ref.shape[0]):
        u_ref[s] = acc[:, 2 * KV_W + s * SLAB:2 * KV_W + (s + 1) * SLAB]
    _rows_to_slab(u_ref, us_ref)


def _ctxproj(ctx2, sc, sh, w_in_bf, tm):
    n_tok, d = ctx2.shape
    tn = Q_W
    n_slab = (tn - 2 * KV_W) // SLAB
    vec = pl.BlockSpec((1, d), lambda i: (0, 0))
    return pl.pallas_call(
        _ctxproj_kernel,
        grid=(n_tok // tm,),
        in_specs=[pl.BlockSpec((tm, d), lambda i: (i, 0)), vec, vec,
                  pl.BlockSpec((d, tn), lambda i: (0, 1))],
        out_specs=[pl.BlockSpec((tm, 2 * KV_W), lambda i: (i, 0)),
                   pl.BlockSpec((n_slab, tm // S5_CHUNK, SLAB_W), lambda i: (0, 0, i))],
        out_shape=[jax.ShapeDtypeStruct((n_tok, 2 * KV_W), BF16),
                   jax.ShapeDtypeStruct((n_slab, tm // S5_CHUNK, (n_tok // tm) * SLAB_W), F32)],
        scratch_shapes=[pltpu.VMEM((n_slab, tm, SLAB), F32)],
        compiler_params=_params("arbitrary"),
        name="ctxproj",
    )(ctx2, sc, sh, w_in_bf)


def _attn_kernel(sink_ref, q_ref, kv_ref, ckv_ref, bias_ref, o_ref, *, seq):
    n = pl.program_id(1)
    start = pl.multiple_of(jnp.clip((n - 1) * BLOCK, 0, seq - BAND), BLOCK)
    keys = jnp.concatenate([kv_ref[pl.ds(start, BAND), :], ckv_ref[...]], 0)
    bias = bias_ref[0]
    ones = jnp.ones((keys.shape[0], HEAD_DIM), BF16)
    for kv in range(N_KV_HEADS):
        heads = [kv * GROUP + g for g in range(GROUP)]
        k = keys[:, kv * HEAD_DIM:(kv + 1) * HEAD_DIM]
        v1 = jnp.concatenate([keys[:, KV_W + kv * HEAD_DIM:KV_W + (kv + 1) * HEAD_DIM], ones], 1)
        q = jnp.concatenate([q_ref[:, h * HEAD_DIM:(h + 1) * HEAD_DIM] for h in heads], 0)
        s = lax.dot_general(q, k, NT_DIMS, preferred_element_type=F32) + bias
        sink = jnp.concatenate([jnp.full((BLOCK, 1), sink_ref[h], F32) for h in heads], 0)
        m = jnp.maximum(jnp.max(s, -1, keepdims=True), sink)
        p = jnp.exp(s - m).astype(BF16)
        pv = jnp.dot(p, v1, preferred_element_type=F32)
        o = pv[:, :HEAD_DIM] / (pv[:, HEAD_DIM:] + jnp.exp(sink - m))
        for g, h in enumerate(heads):
            o_ref[:, h * HEAD_DIM:(h + 1) * HEAD_DIM] = o[g * BLOCK:(g + 1) * BLOCK].astype(o_ref.dtype)


def _attn_bias(n_ctx_tok):
    ql = jnp.arange(BLOCK)[:, None]
    col = jnp.arange(BAND)[None, :]
    rel = jnp.array([0, -BLOCK, -2 * BLOCK])[:, None, None]
    ok = jnp.abs(ql - (rel + col)) <= WINDOW
    ok = jnp.concatenate([ok, jnp.ones((3, BLOCK, n_ctx_tok), bool)], -1)
    return jnp.tile(jnp.where(ok, 0.0, NEG_INF).astype(F32), (1, GROUP, 1))


def _attn(sink, q, kv, ckv, bsz, seq, n_ctx_tok):
    nb = seq // BLOCK
    n_keys = BAND + n_ctx_tok

    def placement(b, n, s):
        return (jnp.where(n == 0, 0, jnp.where(n == nb - 1, 2, 1)), 0, 0)

    return pl.pallas_call(
        functools.partial(_attn_kernel, seq=seq),
        grid_spec=pltpu.PrefetchScalarGridSpec(
            num_scalar_prefetch=1,
            grid=(bsz, nb),
            in_specs=[pl.BlockSpec((BLOCK, Q_W), lambda b, n, s: (b * nb + n, 0)),
                      pl.BlockSpec((seq, 2 * KV_W), lambda b, n, s: (b, 0)),
                      pl.BlockSpec((n_ctx_tok, 2 * KV_W), lambda b, n, s: (b, 0)),
                      pl.BlockSpec((1, GROUP * BLOCK, n_keys), placement)],
            out_specs=pl.BlockSpec((BLOCK, Q_W), lambda b, n, s: (b * nb + n, 0)),
        ),
        out_shape=jax.ShapeDtypeStruct((bsz * seq, Q_W), BF16),
        compiler_params=_params("arbitrary", "arbitrary"),
        name="attn",
    )(sink, q, kv, ckv, _attn_bias(n_ctx_tok))


def _gelu_tanh(x):
    return 0.5 * x * (1.0 + jnp.tanh(0.7978845608028654 * (x + 0.044715 * (x * x * x))))


def _mix_kernel(a_ref, ys_ref, ga_ref, gs_ref, wg_ref, wa_ref, ws_ref, o_ref, y_ref, *, ssm_w):
    _slab_to_rows(ys_ref, y_ref)
    y = jnp.concatenate([y_ref[s] for s in range(y_ref.shape[0])], 1)
    z = jnp.dot(_gelu_tanh(y).astype(BF16), wg_ref[...], preferred_element_type=F32)
    glu = (z[:, :ssm_w] * _sigmoid(z[:, ssm_w:])).astype(BF16)
    attn_d = jnp.dot(a_ref[...], wa_ref[...], preferred_element_type=F32)
    ssm_d = jnp.dot(glu, ws_ref[...], preferred_element_type=F32)
    o_ref[...] = (ga_ref[...] * attn_d + gs_ref[...] * ssm_d).astype(o_ref.dtype)


def _mix(attn, y_slabs, gates, w_glu_bf, w_au_bf, w_su_bf, tm, d, seq):
    n_tok = attn.shape[0]
    n_slab = y_slabs.shape[0]
    ssm_w = n_slab * SLAB
    per_b = seq // tm
    full = lambda a: pl.BlockSpec(a.shape, lambda i: (0, 0))
    return pl.pallas_call(
        functools.partial(_mix_kernel, ssm_w=ssm_w),
        grid=(n_tok // tm,),
        in_specs=[pl.BlockSpec((tm, Q_W), lambda i: (i, 0)),
                  pl.BlockSpec((n_slab, tm // S5_CHUNK, SLAB_W), lambda i: (0, i % per_b, i // per_b)),
                  pl.BlockSpec((tm, d), lambda i: (i, 0)),
                  pl.BlockSpec((tm, d), lambda i: (i, 1)),
                  full(w_glu_bf), full(w_au_bf), full(w_su_bf)],
        out_specs=pl.BlockSpec((tm, d), lambda i: (i, 0)),
        out_shape=jax.ShapeDtypeStruct((n_tok, d), BF16),
        scratch_shapes=[pltpu.VMEM((n_slab, tm, SLAB), F32)],
        compiler_params=_params("arbitrary"),
        name="mix",
    )(attn, y_slabs, gates, gates, w_glu_bf, w_au_bf, w_su_bf)


def _outproj_kernel(m_ref, x_ref, w_ref, g1_ref, lg_ref, lb_ref, sc_ref, sh_ref, x1_ref, h2_ref, *, alpha):
    y = jnp.dot(m_ref[...], w_ref[...], preferred_element_type=F32)
    x1 = _layer_norm(alpha * x_ref[...] + g1_ref[0] * y) * lg_ref[...] + lb_ref[...]
    x1_ref[...] = x1
    h2_ref[...] = (_layer_norm(x1) * (1.0 + sc_ref[0]) + sh_ref[0]).astype(h2_ref.dtype)


def _outproj(mixed, x2, w_out_bf, g1, ln_g, ln_b, sc2, sh2, tm, seq, alpha):
    n_tok, d = x2.shape
    per_b = seq // tm
    rows = pl.BlockSpec((tm, d), lambda i: (i, 0))
    mod_spec = pl.BlockSpec((1, 1, d), lambda i: (i // per_b, 0, 0))
    vec = pl.BlockSpec((1, d), lambda i: (0, 0))
    return pl.pallas_call(
        functools.partial(_outproj_kernel, alpha=alpha),
        grid=(n_tok // tm,),
        in_specs=[rows, rows, pl.BlockSpec((d, d), lambda i: (0, 0)), mod_spec, vec, vec, mod_spec, mod_spec],
        out_specs=[rows, rows],
        out_shape=[jax.ShapeDtypeStruct((n_tok, d), F32), jax.ShapeDtypeStruct((n_tok, d), BF16)],
        compiler_params=_params("arbitrary"),
        name="outproj",
    )(mixed, x2, w_out_bf, g1, ln_g, ln_b, sc2, sh2)


def _mlp_kernel(h_ref, x_ref, w1_ref, b1_ref, w2_ref, b2_ref, g2_ref, lg_ref, lb_ref, o_ref, acc_ref, *, alpha):
    f = pl.program_id(1)
    a = jnp.dot(h_ref[...], w1_ref[...], preferred_element_type=F32) + b1_ref[...]
    a = jnp.maximum(a, 0.0)
    part = jnp.dot((a * a).astype(BF16), w2_ref[...], preferred_element_type=F32)

    @pl.when(f == 0)
    def _():
        acc_ref[...] = part

    @pl.when(f > 0)
    def _():
        acc_ref[...] += part

    @pl.when(f == pl.num_programs(1) - 1)
    def _():
        y = acc_ref[...] + b2_ref[...]
        o_ref[...] = _layer_norm(alpha * x_ref[...] + g2_ref[0] * y) * lg_ref[...] + lb_ref[...]


def _mlp(h2, x1, w1_bf, b1, w2_bf, b2, g2, ln_g, ln_b, tm, tf, seq, alpha):
    n_tok, d = x1.shape
    d_ff = w1_bf.shape[1]
    per_b = seq // tm
    rows = pl.BlockSpec((tm, d), lambda i, f: (i, 0))
    vec = pl.BlockSpec((1, d), lambda i, f: (0, 0))
    return pl.pallas_call(
        functools.partial(_mlp_kernel, alpha=alpha),
        grid=(n_tok // tm, d_ff // tf),
        in_specs=[rows, rows,
                  pl.BlockSpec((d, tf), lambda i, f: (0, f)),
                  pl.BlockSpec((1, tf), lambda i, f: (0, f)),
                  pl.BlockSpec((tf, d), lambda i, f: (f, 0)),
                  vec, pl.BlockSpec((1, 1, d), lambda i, f: (i // per_b, 0, 0)), vec, vec],
        out_specs=rows,
        out_shape=jax.ShapeDtypeStruct((n_tok, d), F32),
        scratch_shapes=[pltpu.VMEM((tm, d), F32)],
        compiler_params=_params("arbitrary", "arbitrary"),
        name="mlp",
    )(h2, x1, w1_bf, b1, w2_bf, b2, g2, ln_g, ln_b)


def _rope_tables(seq):
    rows = seq // GRID_W
    row = jnp.repeat(jnp.arange(rows), GRID_W)
    col = jnp.tile(jnp.arange(GRID_W), rows)
    n_freq = ROT_HALF // 2
    freqs = ROPE_BASE ** (-jnp.arange(n_freq, dtype=F32) / n_freq)
    ang_r = row.astype(F32)[:, None] * freqs
    ang_c = col.astype(F32)[:, None] * freqs
    ang = jnp.concatenate([ang_r, ang_r, ang_c, ang_c], -1)
    cos, sin = jnp.cos(ang), jnp.sin(ang)
    low = (jnp.arange(HEAD_DIM) % ROT_HALF) < n_freq
    return cos, jnp.where(low, -sin, 0.0), jnp.where(low, 0.0, sin)


def _tiles(seq):
    return dict(mod_tn=1024, proj_tm=min(1024, seq), row_tm=min(512, seq), mlp_tf=1024, slab_tc=64)


def kernel(x, c, ctx, c_ctx, w_ada, b_ada, w_in, attn_sink, ssm_a_re, ssm_a_im, ssm_log_dt, ssm_b_re, ssm_b_im,
           ssm_c_re, ssm_c_im, ssm_d, w_glu, w_attn_up, w_ssm_up, w_out, ln_mix_g, ln_mix_b, w_mlp1, b_mlp1,
           w_mlp2, b_mlp2, ln_mlp_g, ln_mlp_b):
    depth = w_ada.shape[0]
    assert depth == 1, "single-layer problem: the context stream is never updated"
    bsz, seq, d = x.shape
    n_ctx_tok = ctx.shape[1]
    groups = ssm_a_re.shape[2]
    ssm_w = groups * SSM_GROUP
    assert w_in.shape[2] == Q_W + 2 * KV_W + ssm_w + 2 * d and Q_W + 2 * KV_W + ssm_w == d
    assert Q_W == 2 * KV_W + ssm_w and d % Q_W == 0, "w_in column tiles: q | k,v,u | gates"
    assert seq % BLOCK == 0 and seq >= BAND and n_ctx_tok % S5_CHUNK == 0 and bsz == 8
    alpha = (2.0 * depth) ** 0.25
    t = _tiles(seq)

    pad = jnp.zeros((16 - bsz - 1, d), F32)
    mod = _mod(jnp.concatenate([c, c_ctx[None], pad], 0), w_ada[0], b_ada[0][None], t["mod_tn"])
    sh1, sc1, g1, sh2, sc2, g2 = [mod[:bsz, i * d:(i + 1) * d][:, None, :] for i in range(6)]
    csh1, csc1 = mod[bsz:bsz + 1, :d], mod[bsz:bsz + 1, d:2 * d]

    w_in_bf = w_in[0].astype(BF16)
    x2 = x.reshape(bsz * seq, d)
    cos, slo, shi = _rope_tables(seq)
    q, kv, gates, u_lat = _inproj(x2, sc1, sh1, w_in_bf, cos, slo, shi, t["proj_tm"], seq)
    ckv, u_ctx = _ctxproj(ctx.reshape(bsz * n_ctx_tok, d), csc1, csh1, w_in_bf, n_ctx_tok)

    attn = _attn(attn_sink[0], q, kv, ckv, bsz, seq, n_ctx_tok)

    m, e, f, l16 = _s5prep(ssm_a_re[0], ssm_a_im[0], ssm_log_dt[0], ssm_b_re[0], ssm_b_im[0],
                           ssm_c_re[0], ssm_c_im[0], ssm_d[0])
    ut_ctx = _split(u_ctx, t["slab_tc"], bsz)
    ut_lat = _split(u_lat, t["slab_tc"], bsz)
    y_slabs = _merge(_s5(ut_ctx, ut_lat, m, e, f, l16, bsz), t["slab_tc"], bsz)

    mixed = _mix(attn, y_slabs, gates, w_glu[0].astype(BF16), w_attn_up[0].astype(BF16), w_ssm_up[0].astype(BF16),
                 t["row_tm"], d, seq)
    x1, h2 = _outproj(mixed, x2, w_out[0].astype(BF16), g1, ln_mix_g, ln_mix_b, sc2, sh2, t["row_tm"], seq, alpha)
    out = _mlp(h2, x1, w_mlp1[0].astype(BF16), b_mlp1, w_mlp2[0].astype(BF16), b_mlp2, g2, ln_mlp_g, ln_mlp_b,
               t["row_tm"], t["mlp_tf"], seq, alpha)
    return out.reshape(bsz, seq, d)
```
